```python
import math
import jax
import jax.numpy as jnp
from jax import lax
import numpy as np

D_MODEL = 1024
BATCH = 1
SEQ = 16384
DEPTH = 2

GRID_W = 64
CTX_LEN = 256
D_HEAD = 64
ROPE_THETA = 10000.0
EPS = 1e-6
QBLOCK = 128
NEG_INF = -1e30

NA_HEADS = 4
NA_KH = 8
NA_KW = 16
DIFF_HEADS = 4
DIFF_QK_DIM = 32
SWA_HEADS = 4
SWA_KV_HEADS = 2
SWA_WINDOW = 128
GQA_HEADS = 4
GQA_KV_HEADS = 2

N_BRANCH = 4
BRANCH_W = 4 * D_HEAD

IN_WIDTHS = (NA_HEADS * D_HEAD, NA_HEADS * D_HEAD, NA_HEADS * D_HEAD,
             DIFF_HEADS * 2 * DIFF_QK_DIM, DIFF_HEADS * 2 * DIFF_QK_DIM, DIFF_HEADS * D_HEAD,
             SWA_HEADS * D_HEAD, SWA_KV_HEADS * D_HEAD, SWA_KV_HEADS * D_HEAD,
             GQA_HEADS * D_HEAD, GQA_KV_HEADS * D_HEAD, GQA_KV_HEADS * D_HEAD)
IN_WIDTH = sum(IN_WIDTHS)

PEER_HEADS = 8
N_KEYS = 128
N_EXPERTS = N_KEYS * N_KEYS
PEER_KEY_DIM = 128
PEER_TOPK = 16
PEER_BLOCK = 128

kernel_name = 'hybrid_diffusion_trunk'


def rmsnorm(x, w):
    xf = x.astype(jnp.float32)
    y = xf * lax.rsqrt(jnp.mean(xf * xf, axis=-1, keepdims=True) + EPS)
    return (y * w.astype(jnp.float32)).astype(x.dtype)


def modulate(h, shift, scale):
    return h * (1 + scale) + shift


def axial_rope_tables(n_tokens, dim):
    t = jnp.arange(n_tokens, dtype=jnp.int32)
    row = (t // GRID_W).astype(jnp.float32)
    col = (t % GRID_W).astype(jnp.float32)
    n_freq = dim // 4
    inv_freq = ROPE_THETA ** (-jnp.arange(n_freq, dtype=jnp.float32) / n_freq)
    ang_r = row[:, None] * inv_freq[None, :]
    ang_c = col[:, None] * inv_freq[None, :]
    ang = jnp.concatenate([ang_r, ang_r, ang_c, ang_c], axis=-1)
    return jnp.cos(ang), jnp.sin(ang)


def apply_axial_rope(x, cos, sin):
    shape = (1, cos.shape[0]) + (1,) * (x.ndim - 3) + (cos.shape[1],)
    cos = cos.reshape(shape).astype(x.dtype)
    sin = sin.reshape(shape).astype(x.dtype)
    x1, x2, x3, x4 = jnp.split(x, 4, axis=-1)
    rot = jnp.concatenate([-x2, x1, -x4, x3], axis=-1)
    return x * cos + rot * sin


def attend(q, k, v, sink=None):
    B, Lq, Hq, Dh = q.shape
    Hkv = k.shape[2]
    G = Hq // Hkv
    qg = q.reshape(B, Lq, Hkv, G, Dh)
    s = jnp.einsum('bqhgd,bkhd->bhgqk', qg, k).astype(jnp.float32) * (Dh ** -0.5)
    if sink is not None:
        sk = jnp.broadcast_to(sink.astype(jnp.float32).reshape(1, Hkv, G, 1, 1), s.shape[:-1] + (1,))
        s = jnp.concatenate([s, sk], axis=-1)
    p = jax.nn.softmax(s, axis=-1)
    if sink is not None:
        p = p[..., :-1]
    o = jnp.einsum('bhgqk,bkhd->bqhgd', p.astype(v.dtype), v)
    return o.reshape(B, Lq, Hq * Dh)


def diff_attend(q, k, v, lam):
    s = jnp.einsum('bqhmd,bkhmd->bhmqk', q, k).astype(jnp.float32) * (q.shape[-1] ** -0.5)
    p = jax.nn.softmax(s, axis=-1)
    a = p[:, :, 0] - lam * p[:, :, 1]
    return jnp.einsum('bhqk,bkhd->bqhd', a.astype(v.dtype), v)


def blocked_queries(fn, q):
    B, S = q.shape[:2]
    nb = S // QBLOCK
    qb = jnp.moveaxis(q.reshape((B, nb, QBLOCK) + q.shape[2:]), 1, 0)
    out = jnp.moveaxis(lax.map(fn, qb), 0, 1)
    return out.reshape((B, S) + out.shape[3:])


def neighbourhood_attention(q, k, v, kc, vc, rpb):
    B, S, H, Dh = q.shape
    rows = S // GRID_W
    kh = min(NA_KH, rows)
    r = jnp.arange(rows)
    row_start = jnp.clip(r - kh // 2, 0, rows - kh)
    band_rows = row_start[:, None] + jnp.arange(kh)[None, :]
    qg = q.reshape(B, rows, GRID_W, H, Dh)
    kg = k.reshape(B, rows, GRID_W, H, Dh)[:, band_rows]
    vg = v.reshape(B, rows, GRID_W, H, Dh)[:, band_rows]
    scale = Dh ** -0.5
    s_loc = jnp.einsum('brqhd,brjchd->bhrqjc', qg, kg).astype(jnp.float32) * scale
    qcol = jnp.arange(GRID_W)
    col_start = jnp.clip(qcol - NA_KW // 2, 0, GRID_W - NA_KW)
    col_ok = (qcol[None, :] >= col_start[:, None]) & (qcol[None, :] < col_start[:, None] + NA_KW)
    roff = band_rows - r[:, None] + (NA_KH - 1)
    coff = jnp.clip(qcol[None, :] - qcol[:, None], -(NA_KW - 1), NA_KW - 1) + (NA_KW - 1)
    bias = rpb[:, roff[:, None, :, None], coff[None, :, None, :]]
    s_loc = jnp.where(col_ok[None, None, None, :, None, :], s_loc + bias[None].astype(jnp.float32), NEG_INF)
    s_loc = s_loc.reshape(B, H, rows, GRID_W, kh * GRID_W)
    s_ctx = jnp.einsum('brqhd,bkhd->bhrqk', qg, kc).astype(jnp.float32) * scale
    p = jax.nn.softmax(jnp.concatenate([s_loc, s_ctx], axis=-1), axis=-1).astype(v.dtype)
    n_loc = kh * GRID_W
    vg = vg.reshape(B, rows, n_loc, H, Dh)
    o = (jnp.einsum('bhrqk,brkhd->brqhd', p[..., :n_loc], vg)
         + jnp.einsum('bhrqk,bkhd->brqhd', p[..., n_loc:], vc))
    return o.reshape(B, S, H * Dh)


def swa_attention(q, k, v, kc, vc, sink):
    B, S, Hq, Dh = q.shape
    Hkv = k.shape[2]
    G = Hq // Hkv
    nb = S // QBLOCK
    span = QBLOCK + 2 * SWA_WINDOW
    pad = ((0, 0), (SWA_WINDOW, SWA_WINDOW), (0, 0), (0, 0))
    kp = jnp.pad(k, pad)
    vp = jnp.pad(v, pad)
    idx = jnp.arange(nb)[:, None] * QBLOCK + jnp.arange(span)[None, :]
    kb = kp[:, idx]
    vb = vp[:, idx]
    qb = q.reshape(B, nb, QBLOCK, Hkv, G, Dh)
    scale = Dh ** -0.5
    s_loc = jnp.einsum('bnqhgd,bnkhd->bhgnqk', qb, kb).astype(jnp.float32) * scale
    qpos = jnp.arange(nb)[:, None] * QBLOCK + jnp.arange(QBLOCK)[None, :]
    kpos = idx - SWA_WINDOW
    ok = ((kpos[:, None, :] >= 0) & (kpos[:, None, :] < S)
          & (jnp.abs(qpos[:, :, None] - kpos[:, None, :]) <= SWA_WINDOW))
    s_loc = jnp.where(ok[None, None, None], s_loc, NEG_INF)
    s_ctx = jnp.einsum('bnqhgd,bkhd->bhgnqk', qb, kc).astype(jnp.float32) * scale
    s_sink = jnp.broadcast_to(sink.astype(jnp.float32).reshape(1, Hkv, G, 1, 1, 1), s_loc.shape[:-1] + (1,))
    p = jax.nn.softmax(jnp.concatenate([s_loc, s_ctx, s_sink], axis=-1), axis=-1).astype(v.dtype)
    C = kc.shape[1]
    o = (jnp.einsum('bhgnqk,bnkhd->bnqhgd', p[..., :span], vb)
         + jnp.einsum('bhgnqk,bkhd->bnqhgd', p[..., span:span + C], vc))
    return o.reshape(B, S, Hq * Dh)


def project_heads(h, w_in):
    B, L, _ = h.shape
    offsets = [int(o) for o in np.cumsum(IN_WIDTHS)[:-1]]
    (na_q, na_k, na_v, df_q, df_k, df_v, sw_q, sw_k, sw_v,
     gq_q, gq_k, gq_v) = jnp.split(h @ w_in, offsets, axis=-1)
    hd = lambda t, n: t.reshape(B, L, n, D_HEAD)
    dq = lambda t: t.reshape(B, L, DIFF_HEADS, 2, DIFF_QK_DIM)
    return (hd(na_q, NA_HEADS), hd(na_k, NA_HEADS), hd(na_v, NA_HEADS),
            dq(df_q), dq(df_k), hd(df_v, DIFF_HEADS),
            hd(sw_q, SWA_HEADS), hd(sw_k, SWA_KV_HEADS), hd(sw_v, SWA_KV_HEADS),
            hd(gq_q, GQA_HEADS), hd(gq_k, GQA_KV_HEADS), hd(gq_v, GQA_KV_HEADS))


def diff_subln(o, w, lam_init):
    B, L, H, Dh = o.shape
    return (rmsnorm(o, w) * (1.0 - lam_init)).reshape(B, L, H * Dh)


def token_mixers(h, hc, w_in, rpb, lam_p, subln_w, sink, qk_norm_w, lam_init, cos, sin, cos_d, sin_d, need_ctx):
    (na_q, na_k, na_v, df_q, df_k, df_v, sw_q, sw_k, sw_v, gq_q, gq_k, gq_v) = project_heads(h, w_in)
    (na_qc, na_kc, na_vc, df_qc, df_kc, df_vc, sw_qc, sw_kc, sw_vc, gq_qc, gq_kc, gq_vc) = project_heads(hc, w_in)
    lf = lam_p.astype(jnp.float32)
    lam = jnp.exp(jnp.sum(lf[0] * lf[1])) - jnp.exp(jnp.sum(lf[2] * lf[3])) + lam_init

    y_na = neighbourhood_attention(na_q, na_k, na_v, na_kc, na_vc, rpb)

    dk_all = jnp.concatenate([apply_axial_rope(df_k, cos_d, sin_d), df_kc], axis=1)
    dv_all = jnp.concatenate([df_v, df_vc], axis=1)
    y_df = blocked_queries(lambda qb: diff_attend(qb, dk_all, dv_all, lam), apply_axial_rope(df_q, cos_d, sin_d))
    y_df = diff_subln(y_df, subln_w, lam_init)

    y_sw = swa_attention(apply_axial_rope(sw_q, cos, sin), apply_axial_rope(sw_k, cos, sin), sw_v, sw_kc, sw_vc, sink)

    gq_q = apply_axial_rope(rmsnorm(gq_q, qk_norm_w[0]), cos, sin)
    gk_c = rmsnorm(gq_kc, qk_norm_w[1])
    gk_all = jnp.concatenate([apply_axial_rope(rmsnorm(gq_k, qk_norm_w[1]), cos, sin), gk_c], axis=1)
    gv_all = jnp.concatenate([gq_v, gq_vc], axis=1)
    y_gq = blocked_queries(lambda qb: attend(qb, gk_all, gv_all), gq_q)

    ys = jnp.stack([y_na, y_df, y_sw, y_gq], axis=2)
    if not need_ctx:
        return ys, None
    yc_na = attend(na_qc, na_kc, na_vc)
    yc_df = diff_subln(diff_attend(df_qc, df_kc, df_vc, lam), subln_w, lam_init)
    yc_sw = attend(sw_qc, sw_kc, sw_vc, sink)
    yc_gq = attend(rmsnorm(gq_qc, qk_norm_w[0]), gk_c, gq_vc)
    ysc = jnp.stack([yc_na, yc_df, yc_sw, yc_gq], axis=2)
    return ys, ysc


def merge_branches(h, ys, w_branch, w_gate, b_gate, w_out):
    B, L, D = h.shape
    gates = jax.nn.sigmoid(h @ w_gate + b_gate).reshape(B, L, N_BRANCH, D)
    br = jnp.einsum('blnw,nwd->blnd', ys, w_branch)
    return jnp.sum(gates * br, axis=2) @ w_out


def peer_ffn(h, wq, keys, u, v):
    B, L, D = h.shape
    T = B * L
    hf = h.reshape(T, D)
    q = (hf @ wq).reshape(T, PEER_HEADS, 2, PEER_KEY_DIM)
    s = jnp.einsum('thpd,phnd->thpn', q, keys).astype(jnp.float32)
    top_s, top_i = lax.top_k(s, PEER_TOPK)
    cand_s = (top_s[:, :, 0, :, None] + top_s[:, :, 1, None, :]).reshape(T, PEER_HEADS, PEER_TOPK * PEER_TOPK)
    cand_i = (top_i[:, :, 0, :, None] * N_KEYS + top_i[:, :, 1, None, :]).reshape(T, PEER_HEADS, PEER_TOPK * PEER_TOPK)
    best_s, pos = lax.top_k(cand_s, PEER_TOPK)
    expert = jnp.take_along_axis(cand_i, pos, axis=-1)
    g = jax.nn.softmax(best_s, axis=-1).astype(h.dtype)
    nb = T // PEER_BLOCK

    def one_block(args):
        xb, eb, gb = args
        a = jax.nn.gelu(jnp.einsum('td,thkd->thk', xb, u[eb]))
        return jnp.einsum('thk,thkd->td', gb * a, v[eb])

    out = lax.map(one_block, (hf.reshape(nb, PEER_BLOCK, D),
                              expert.reshape(nb, PEER_BLOCK, PEER_HEADS, PEER_TOPK),
                              g.reshape(nb, PEER_BLOCK, PEER_HEADS, PEER_TOPK)))
    return out.reshape(B, L, D)


def setup_inputs(seed: int = 0) -> dict:
    key = jax.random.key(seed)
    ks = jax.random.split(key, 24)
    nrm = lambda k, shape, s: jax.random.normal(k, shape, jnp.float32) * s
    L = DEPTH
    D = D_MODEL
    return {
        'x': nrm(ks[0], (BATCH, SEQ, D), 1.0),
        'c': nrm(ks[1], (BATCH, D), 1.0),
        'ctx': nrm(ks[2], (BATCH, CTX_LEN, D), 1.0),
        'c_ctx': nrm(ks[3], (D,), 1.0),
        'norm1_w': 1.0 + nrm(ks[4], (L, D), 0.02),
        'norm2_w': 1.0 + nrm(ks[5], (L, D), 0.02),
        'ada_w': nrm(ks[6], (L, D, 6 * D), 0.5 * D ** -0.5),
        'ada_b': nrm(ks[7], (L, 6 * D), 0.01),
        'w_in': nrm(ks[8], (L, D, IN_WIDTH), D ** -0.5),
        'na_rpb': nrm(ks[9], (L, NA_HEADS, 2 * NA_KH - 1, 2 * NA_KW - 1), 0.02),
        'diff_lam': nrm(ks[10], (L, 4, DIFF_QK_DIM), 0.1),
        'diff_subln_w': 1.0 + nrm(ks[11], (L, D_HEAD), 0.02),
        'swa_sink': nrm(ks[12], (L, SWA_HEADS), 0.5),
        'gqa_qk_norm_w': 1.0 + nrm(ks[13], (L, 2, D_HEAD), 0.02),
        'w_branch': nrm(ks[14], (L, N_BRANCH, BRANCH_W, D), BRANCH_W ** -0.5),
        'w_gate': nrm(ks[15], (L, D, N_BRANCH * D), D ** -0.5),
        'b_gate': nrm(ks[16], (L, N_BRANCH * D), 0.01),
        'w_out': nrm(ks[17], (L, D, D), D ** -0.5),
        'peer_wq': nrm(ks[18], (L, D, PEER_HEADS * 2 * PEER_KEY_DIM), D ** -0.5),
        'peer_keys': nrm(ks[19], (L, 2, PEER_HEADS, N_KEYS, PEER_KEY_DIM), PEER_KEY_DIM ** -0.5),
        'peer_u': nrm(ks[20], (L, N_EXPERTS, D), D ** -0.5),
        'peer_v': nrm(ks[21], (L, N_EXPERTS, D), PEER_HEADS ** -0.5),
        'final_norm_w': 1.0 + nrm(ks[22], (D,), 0.02),
    }


def reference(x, c, ctx, c_ctx, norm1_w, norm2_w, ada_w, ada_b, w_in, na_rpb, diff_lam, diff_subln_w,
              swa_sink, gqa_qk_norm_w, w_branch, w_gate, b_gate, w_out, peer_wq, peer_keys, peer_u, peer_v,
              final_norm_w):
    S = x.shape[1]
    cos, sin = axial_rope_tables(S, D_HEAD)
    cos_d, sin_d = axial_rope_tables(S, DIFF_QK_DIM)
    xc = ctx
    for l in range(DEPTH):
        need_ctx = l < DEPTH - 1
        lam_init = 0.8 - 0.6 * math.exp(-0.3 * l)
        mod = jax.nn.silu(c) @ ada_w[l] + ada_b[l]
        sh1, sc1, g1, sh2, sc2, g2 = [m[:, None, :] for m in jnp.split(mod, 6, axis=-1)]
        modc = jax.nn.silu(c_ctx) @ ada_w[l] + ada_b[l]
        sh1c, sc1c, g1c, sh2c, sc2c, g2c = jnp.split(modc, 6, axis=-1)

        h = modulate(rmsnorm(x, norm1_w[l]), sh1, sc1)
        hc = modulate(rmsnorm(xc, norm1_w[l]), sh1c, sc1c)
        ys, ysc = token_mixers(h, hc, w_in[l], na_rpb[l], diff_lam[l], diff_subln_w[l], swa_sink[l],
                               gqa_qk_norm_w[l], lam_init, cos, sin, cos_d, sin_d, need_ctx)
        x = x + g1 * merge_branches(h, ys, w_branch[l], w_gate[l], b_gate[l], w_out[l])
        if need_ctx:
            xc = xc + g1c * merge_branches(hc, ysc, w_branch[l], w_gate[l], b_gate[l], w_out[l])

        h2 = modulate(rmsnorm(x, norm2_w[l]), sh2, sc2)
        x = x + g2 * peer_ffn(h2, peer_wq[l], peer_keys[l], peer_u[l], peer_v[l])
        if need_ctx:
            h2c = modulate(rmsnorm(xc, norm2_w[l]), sh2c, sc2c)
            xc = xc + g2c * peer_ffn(h2c, peer_wq[l], peer_keys[l], peer_u[l], peer_v[l])
    return rmsnorm(x, final_norm_w)
```

```python
import functools
import math

import numpy as np
import jax
import jax.numpy as jnp
from jax import lax
from jax.experimental import pallas as pl
from jax.experimental.pallas import tpu as pltpu

F32 = jnp.float32
BF16 = jnp.bfloat16

D_MODEL = 1024
GRID_W = 64
D_HEAD = 64
ROPE_THETA = 10000.0
EPS = 1e-6
NEG_INF = -1e30
NA_KH = 8
NA_KW = 16
DIFF_QK_DIM = 32
SWA_WINDOW = 128
N_BRANCH = 4
GROUP_W = 256
N_GROUPS = 10
PEER_HEADS = 8
N_KEYS = 128
PEER_KEY_DIM = 128
PEER_TOPK = 16
QBLK = 128
NA_KBLKS = 5
VMEM_LIMIT = 56 * 1024 * 1024


def _dot(a, b):
    return jnp.dot(a, b, preferred_element_type=F32)


def _dot_nt(a, b):
    return lax.dot_general(a, b, (((1,), (1,)), ((), ())), preferred_element_type=F32)


def _params(sem, vmem=VMEM_LIMIT):
    return pltpu.CompilerParams(dimension_semantics=sem, vmem_limit_bytes=vmem)


def _const_spec(shape, index_map):
    return pl.BlockSpec(shape, index_map, pipeline_mode=pl.Buffered(1))


def _mod_rows(mod_ref, k, is_ctx):
    lo, hi = k * D_MODEL, (k + 1) * D_MODEL
    return jnp.where(is_ctx, mod_ref[1:2, lo:hi], mod_ref[0:1, lo:hi])


def _norm_modulate(x, nw, shift, scale):
    ms = jnp.mean(x * x, axis=-1, keepdims=True)
    h = (x * lax.rsqrt(ms + EPS)) * nw
    return h * (1.0 + scale) + shift


def _group_mean_sq(y, width):
    w = y.shape[1]
    y2 = y * y
    r = lax.broadcasted_iota(jnp.int32, (w, w), 0) // width
    c = lax.broadcasted_iota(jnp.int32, (w, w), 1) // width
    ones = jnp.where(r == c, 1.0, 0.0).astype(BF16)
    hi = y2.astype(BF16)
    lo = (y2 - hi.astype(F32)).astype(BF16)
    return (_dot(hi, ones) + _dot(lo, ones)) * (1.0 / width)


def _rope(y, cos, sin_signed, quarter):
    w = y.shape[1]
    lane = lax.broadcasted_iota(jnp.int32, y.shape, 1)
    first = (lane % (2 * quarter)) < quarter
    rot = jnp.where(first, pltpu.roll(y, w - quarter, 1), pltpu.roll(y, quarter, 1))
    return y * cos + rot * sin_signed


def _softmax_pv(s, v, sink=None):
    m = jnp.max(s, axis=1, keepdims=True)
    if sink is not None:
        m = jnp.maximum(m, sink)
    p = jnp.exp(s - m)
    l = jnp.sum(p, axis=1, keepdims=True)
    if sink is not None:
        l = l + jnp.exp(sink - m)
    return _dot(p.astype(BF16), v) / l


def _diff_lambda(lam_ref, lam_init):
    lf = lam_ref[...]
    a = jnp.sum(lf[0:1] * lf[1:2], axis=1, keepdims=True)
    b = jnp.sum(lf[2:3] * lf[3:4], axis=1, keepdims=True)
    return jnp.exp(a) - jnp.exp(b) + lam_init


def _diff_finish(o, subw, lam_init):
    ms = _group_mean_sq(o, D_HEAD)
    return (o * lax.rsqrt(ms + EPS)) * subw * (1.0 - lam_init)


def _mod_kernel(c_ref, w_ref, b_ref, o_ref):
    c = c_ref[...]
    s = c * jax.nn.sigmoid(c)
    o_ref[...] = jnp.dot(s, w_ref[...], precision=lax.Precision.HIGHEST,
                         preferred_element_type=F32) + b_ref[...]


def _modulation(c2, ada_w, ada_b, layer):
    n_out = ada_w.shape[2]
    tn = 1536
    return pl.pallas_call(
        _mod_kernel,
        grid=(n_out // tn,),
        in_specs=[pl.BlockSpec((8, D_MODEL), lambda j: (0, 0)),
                  pl.BlockSpec((None, D_MODEL, tn), lambda j: (layer, 0, j)),
                  pl.BlockSpec((None, 1, tn), lambda j: (layer, 0, j))],
        out_specs=pl.BlockSpec((8, tn), lambda j: (0, j)),
        out_shape=jax.ShapeDtypeStruct((8, n_out), F32),
        compiler_params=_params(("arbitrary",)),
        name="adaln_mod",
    )(c2, ada_w, ada_b.reshape(ada_b.shape[0], 1, n_out))


def _proj_kernel(x_ref, mod_ref, nw_ref, w_ref, c64_ref, s64_ref, c32_ref, s32_ref, qkw_ref, o_ref,
                 *, n_lat, tm):
    i = pl.program_id(0)
    row = i * tm + lax.broadcasted_iota(jnp.int32, (tm, 1), 0)
    is_ctx = row >= n_lat
    h = _norm_modulate(x_ref[...], nw_ref[...], _mod_rows(mod_ref, 0, is_ctx), _mod_rows(mod_ref, 1, is_ctx))
    hb = h.astype(BF16)
    c64, s64 = c64_ref[...], s64_ref[...]
    c32, s32 = c32_ref[...], s32_ref[...]
    half = GROUP_W // 2
    sc64 = D_HEAD ** -0.5
    sc32 = DIFF_QK_DIM ** -0.5

    def qk_norm(y, w):
        return (y * lax.rsqrt(_group_mean_sq(y, D_HEAD) + EPS)) * w

    for g in range(N_GROUPS):
        y = _dot(hb, w_ref[:, g * GROUP_W:(g + 1) * GROUP_W])
        if g == 0:
            y = y * sc64
        elif g == 3:
            y = _rope(y, c32, s32, DIFF_QK_DIM // 4) * sc32
        elif g == 4:
            y = _rope(y, c32, s32, DIFF_QK_DIM // 4)
        elif g == 6:
            y = _rope(y, c64, s64, D_HEAD // 4) * sc64
        elif g == 7:
            k = _rope(y[:, :half], c64[:, :half], s64[:, :half], D_HEAD // 4)
            y = jnp.concatenate([k, y[:, half:]], axis=1)
        elif g == 8:
            y = _rope(qk_norm(y, qkw_ref[0:1, :]), c64, s64, D_HEAD // 4) * sc64
        elif g == 9:
            k = _rope(qk_norm(y[:, :half], qkw_ref[1:2, :half]), c64[:, :half], s64[:, :half], D_HEAD // 4)
            y = jnp.concatenate([k, y[:, half:]], axis=1)
        o_ref[:, g * GROUP_W:(g + 1) * GROUP_W] = y.astype(BF16)


def _projection(xs, mod, nw, w_in, tabs, qkw, layer, n_lat):
    n = xs.shape[0]
    tm = _row_tile(n)
    width = N_GROUPS * GROUP_W
    row_spec = lambda w: pl.BlockSpec((tm, w), lambda i: (i, 0))
    return pl.pallas_call(
        functools.partial(_proj_kernel, n_lat=n_lat, tm=tm),
        grid=(n // tm,),
        in_specs=[row_spec(D_MODEL),
                  _const_spec((8, 6 * D_MODEL), lambda i: (0, 0)),
                  _const_spec((None, 1, D_MODEL), lambda i: (layer, 0, 0)),
                  _const_spec((None, D_MODEL, width), lambda i: (layer, 0, 0)),
                  row_spec(GROUP_W), row_spec(GROUP_W), row_spec(GROUP_W), row_spec(GROUP_W),
                  _const_spec((None, 2, GROUP_W), lambda i: (layer, 0, 0))],
        out_specs=row_spec(width),
        out_shape=jax.ShapeDtypeStruct((n, width), BF16),
        compiler_params=_params(("parallel",)),
        name="proj",
    )(xs, mod, nw, w_in, *tabs, qkw)


def _na_kernel(q_ref, *refs):
    k_refs = refs[:NA_KBLKS]
    v_refs = refs[NA_KBLKS:2 * NA_KBLKS]
    kc_ref, vc_ref, bias_ref, o_ref = refs[2 * NA_KBLKS:]
    q = q_ref[...]
    ks = [r[...] for r in k_refs] + [kc_ref[...]]
    vs = [r[...] for r in v_refs] + [vc_ref[...]]
    outs = []
    for h in range(4):
        sl = slice(h * D_HEAD, (h + 1) * D_HEAD)
        qh = q[:, sl]
        s_loc = jnp.concatenate([_dot_nt(qh, k[:, sl]) for k in ks[:NA_KBLKS]], axis=1) + bias_ref[0, h]
        s = jnp.concatenate([s_loc, _dot_nt(qh, ks[NA_KBLKS][:, sl])], axis=1)
        v = jnp.concatenate([v[:, sl] for v in vs], axis=0)
        outs.append(_softmax_pv(s, v))
    o_ref[...] = jnp.concatenate(outs, axis=1).astype(BF16)


def _na_bias(rpb, n_lat):
    nb = n_lat // QBLK
    rows = n_lat // GRID_W
    rep = np.array([0, 1, 2, nb - 2, nb - 1])
    wstart = np.clip(rep - 2, 0, nb - NA_KBLKS)
    qi = np.arange(QBLK)
    kj = np.arange(NA_KBLKS * QBLK)
    qrow = 2 * rep[:, None] + qi[None, :] // GRID_W
    qcol = qi % GRID_W
    krow = 2 * wstart[:, None] + kj[None, :] // GRID_W
    kcol = kj % GRID_W
    rs = np.clip(qrow - NA_KH // 2, 0, rows - NA_KH)
    row_ok = (krow[:, None, :] >= rs[:, :, None]) & (krow[:, None, :] < rs[:, :, None] + NA_KH)
    cs = np.clip(qcol - NA_KW // 2, 0, GRID_W - NA_KW)
    col_ok = (kcol[None, :] >= cs[:, None]) & (kcol[None, :] < cs[:, None] + NA_KW)
    ok = row_ok & col_ok[None]
    ridx = np.clip(krow[:, None, :] - qrow[:, :, None] + (NA_KH - 1), 0, 2 * NA_KH - 2)
    cidx = np.clip(kcol[None, :] - qcol[:, None], -(NA_KW - 1), NA_KW - 1) + (NA_KW - 1)
    cidx = np.broadcast_to(cidx[None], ridx.shape)
    bias = rpb.astype(F32)[:, ridx, cidx]
    return jnp.where(ok[None], bias, NEG_INF).transpose(1, 0, 2, 3)


def _na_attention(qkv, bias, n_lat):
    n = qkv.shape[0]
    nb = n_lat // QBLK
    cblk = n_lat // 256

    def cls(b):
        return jnp.where(b < 2, b, jnp.where(b >= nb - 2, b - (nb - 5), 2))

    def kblk(j, col):
        return pl.BlockSpec((QBLK, GROUP_W), lambda b: (jnp.clip(b - 2, 0, nb - NA_KBLKS) + j, col))

    in_specs = ([pl.BlockSpec((QBLK, GROUP_W), lambda b: (b, 0))]
                + [kblk(j, 1) for j in range(NA_KBLKS)] + [kblk(j, 2) for j in range(NA_KBLKS)]
                + [pl.BlockSpec((256, GROUP_W), lambda b: (cblk, 1)),
                   pl.BlockSpec((256, GROUP_W), lambda b: (cblk, 2)),
                   pl.BlockSpec((1, 4, QBLK, NA_KBLKS * QBLK), lambda b: (cls(b), 0, 0, 0))])
    return pl.pallas_call(
        _na_kernel,
        grid=(nb,),
        in_specs=in_specs,
        out_specs=pl.BlockSpec((QBLK, GROUP_W), lambda b: (b, 0)),
        out_shape=jax.ShapeDtypeStruct((n, GROUP_W), BF16),
        compiler_params=_params(("parallel",)),
        name="na_attn",
    )(*([qkv] * (3 + 2 * NA_KBLKS)), bias)


def _swa_kernel(sink_ref, q_ref, km_ref, k0_ref, kp_ref, kvc_ref, o_ref, *, nb):
    b = pl.program_id(0)
    q = q_ref[...]
    blocks = [km_ref[...], k0_ref[...], kp_ref[...], kvc_ref[...]]
    row = lax.broadcasted_iota(jnp.int32, (QBLK, QBLK), 0)
    col = lax.broadcasted_iota(jnp.int32, (QBLK, QBLK), 1)
    ok_prev = jnp.logical_and(col >= row, b > 0)
    ok_next = jnp.logical_and(col <= row, b < nb - 1)
    outs = []
    for h in range(4):
        g = h // 2
        qh = q[:, h * D_HEAD:(h + 1) * D_HEAD]
        ksl = slice(g * D_HEAD, (g + 1) * D_HEAD)
        vsl = slice(GROUP_W // 2 + g * D_HEAD, GROUP_W // 2 + (g + 1) * D_HEAD)
        s = jnp.concatenate([jnp.where(ok_prev, _dot_nt(qh, blocks[0][:, ksl]), NEG_INF),
                             _dot_nt(qh, blocks[1][:, ksl]),
                             jnp.where(ok_next, _dot_nt(qh, blocks[2][:, ksl]), NEG_INF),
                             _dot_nt(qh, blocks[3][:, ksl])], axis=1)
        v = jnp.concatenate([blk[:, vsl] for blk in blocks], axis=0)
        outs.append(_softmax_pv(s, v, sink=sink_ref[h]))
    o_ref[...] = jnp.concatenate(outs, axis=1).astype(BF16)


def _swa_attention(qkv, sink, n_lat):
    n = qkv.shape[0]
    nb = n_lat // QBLK
    cblk = n_lat // 256
    return pl.pallas_call(
        functools.partial(_swa_kernel, nb=nb),
        grid=(nb,),
        in_specs=[pl.BlockSpec(memory_space=pltpu.SMEM),
                  pl.BlockSpec((QBLK, GROUP_W), lambda b: (b, 6)),
                  pl.BlockSpec((QBLK, GROUP_W), lambda b: (jnp.maximum(b - 1, 0), 7)),
                  pl.BlockSpec((QBLK, GROUP_W), lambda b: (b, 7)),
                  pl.BlockSpec((QBLK, GROUP_W), lambda b: (jnp.minimum(b + 1, nb - 1), 7)),
                  pl.BlockSpec((256, GROUP_W), lambda b: (cblk, 7))],
        out_specs=pl.BlockSpec((QBLK, GROUP_W), lambda b: (b, 0)),
        out_shape=jax.ShapeDtypeStruct((n, GROUP_W), BF16),
        compiler_params=_params(("parallel",)),
        name="swa_attn",
    )(sink, qkv, qkv, qkv, qkv, qkv)


def _flash_update(u, qu, ku, vu, m_ref, l_ref, acc_ref):
    s = _dot_nt(qu, ku)
    m_prev = m_ref[u]
    m_new = jnp.maximum(m_prev, jnp.max(s, axis=1, keepdims=True))
    alpha = jnp.exp(m_prev - m_new)
    p = jnp.exp(s - m_new)
    l_ref[u] = alpha * l_ref[u] + jnp.sum(p, axis=1, keepdims=True)
    acc_ref[u] = alpha * acc_ref[u] + _dot(p.astype(BF16), vu)
    m_ref[u] = m_new


def _flash_init(m_ref, l_ref, acc_ref):
    m_ref[...] = jnp.full(m_ref.shape, NEG_INF, F32)
    l_ref[...] = jnp.zeros(l_ref.shape, F32)
    acc_ref[...] = jnp.zeros(acc_ref.shape, F32)


def _diff_flash_kernel(lam_ref, subw_ref, q_ref, k_ref, v_ref, o_ref, m_ref, l_ref, acc_ref, *, lam_init):
    j = pl.program_id(1)

    @pl.when(j == 0)
    def _():
        _flash_init(m_ref, l_ref, acc_ref)

    q, k, v = q_ref[...], k_ref[...], v_ref[...]
    for u in range(8):
        qs = slice(u * DIFF_QK_DIM, (u + 1) * DIFF_QK_DIM)
        vs = slice((u // 2) * D_HEAD, (u // 2 + 1) * D_HEAD)
        _flash_update(u, q[:, qs], k[:, qs], v[:, vs], m_ref, l_ref, acc_ref)

    @pl.when(j == pl.num_programs(1) - 1)
    def _():
        lam = _diff_lambda(lam_ref, lam_init)
        o = jnp.concatenate([acc_ref[2 * h] / l_ref[2 * h] - lam * (acc_ref[2 * h + 1] / l_ref[2 * h + 1])
                             for h in range(4)], axis=1)
        o_ref[...] = _diff_finish(o, subw_ref[...], lam_init).astype(BF16)


def _gqa_flash_kernel(q_ref, kv_ref, o_ref, m_ref, l_ref, acc_ref):
    j = pl.program_id(1)

    @pl.when(j == 0)
    def _():
        _flash_init(m_ref, l_ref, acc_ref)

    q, kv = q_ref[...], kv_ref[...]
    for h in range(4):
        g = h // 2
        ks = slice(g * D_HEAD, (g + 1) * D_HEAD)
        vs = slice(GROUP_W // 2 + g * D_HEAD, GROUP_W // 2 + (g + 1) * D_HEAD)
        _flash_update(h, q[:, h * D_HEAD:(h + 1) * D_HEAD], kv[:, ks], kv[:, vs], m_ref, l_ref, acc_ref)

    @pl.when(j == pl.num_programs(1) - 1)
    def _():
        o_ref[...] = jnp.concatenate([acc_ref[h] / l_ref[h] for h in range(4)], axis=1).astype(BF16)


def _flash_tiles(n, n_lat):
    tq = 512 if n_lat % 512 == 0 else 256
    tk = 1280 if n % 1280 == 0 else 256
    return tq, tk


def _flash_scratch(units, tq):
    return [pltpu.VMEM((units, tq, 1), F32), pltpu.VMEM((units, tq, 1), F32),
            pltpu.VMEM((units, tq, D_HEAD), F32)]


def _diff_attention(qkv, lam_p, subw, lam_init, n_lat):
    n = qkv.shape[0]
    tq, tk = _flash_tiles(n, n_lat)
    return pl.pallas_call(
        functools.partial(_diff_flash_kernel, lam_init=lam_init),
        grid=(n_lat // tq, n // tk),
        in_specs=[pl.BlockSpec((4, DIFF_QK_DIM), lambda i, j: (0, 0)),
                  pl.BlockSpec((1, GROUP_W), lambda i, j: (0, 0)),
                  pl.BlockSpec((tq, GROUP_W), lambda i, j: (i, 3)),
                  pl.BlockSpec((tk, GROUP_W), lambda i, j: (j, 4)),
                  pl.BlockSpec((tk, GROUP_W), lambda i, j: (j, 5))],
        out_specs=pl.BlockSpec((tq, GROUP_W), lambda i, j: (i, 0)),
        out_shape=jax.ShapeDtypeStruct((n, GROUP_W), BF16),
        scratch_shapes=_flash_scratch(8, tq),
        compiler_params=_params(("parallel", "arbitrary")),
        name="diff_attn",
    )(lam_p, subw, qkv, qkv, qkv)


def _gqa_attention(qkv, n_lat):
    n = qkv.shape[0]
    tq, tk = _flash_tiles(n, n_lat)
    return pl.pallas_call(
        _gqa_flash_kernel,
        grid=(n_lat // tq, n // tk),
        in_specs=[pl.BlockSpec((tq, GROUP_W), lambda i, j: (i, 8)),
                  pl.BlockSpec((tk, GROUP_W), lambda i, j: (j, 9))],
        out_specs=pl.BlockSpec((tq, GROUP_W), lambda i, j: (i, 0)),
        out_shape=jax.ShapeDtypeStruct((n, GROUP_W), BF16),
        scratch_shapes=_flash_scratch(4, tq),
        compiler_params=_params(("parallel", "arbitrary")),
        name="gqa_attn",
    )(qkv, qkv)


def _ctx_kernel(sink_ref, lam_ref, subw_ref, qkv_ref, a0, a1, a2, a3, o_na, o_df, o_sw, o_gq, *, lam_init):
    del a0, a1, a2, a3
    grp = lambda g: qkv_ref[:, g * GROUP_W:(g + 1) * GROUP_W]
    hs = lambda h: slice(h * D_HEAD, (h + 1) * D_HEAD)
    half = GROUP_W // 2

    q, k, v = grp(0), grp(1), grp(2)
    o_na[...] = jnp.concatenate([_softmax_pv(_dot_nt(q[:, hs(h)], k[:, hs(h)]), v[:, hs(h)])
                                 for h in range(4)], axis=1).astype(BF16)

    q, k, v = grp(3), grp(4), grp(5)
    lam = _diff_lambda(lam_ref, lam_init)
    heads = []
    for h in range(4):
        maps = []
        for m in range(2):
            qs = slice((2 * h + m) * DIFF_QK_DIM, (2 * h + m + 1) * DIFF_QK_DIM)
            maps.append(_softmax_pv(_dot_nt(q[:, qs], k[:, qs]), v[:, hs(h)]))
        heads.append(maps[0] - lam * maps[1])
    o_df[...] = _diff_finish(jnp.concatenate(heads, axis=1), subw_ref[...], lam_init).astype(BF16)

    for q_g, kv_g, out, use_sink in ((6, 7, o_sw, True), (8, 9, o_gq, False)):
        q, kv = grp(q_g), grp(kv_g)
        outs = []
        for h in range(4):
            g = h // 2
            s = _dot_nt(q[:, hs(h)], kv[:, hs(g)])
            vh = kv[:, half + g * D_HEAD:half + (g + 1) * D_HEAD]
            outs.append(_softmax_pv(s, vh, sink=sink_ref[h] if use_sink else None))
        out[...] = jnp.concatenate(outs, axis=1).astype(BF16)


def _ctx_attention(qkv, ys, sink, lam_p, subw, lam_init, n_lat):
    n = qkv.shape[0]
    n_ctx = n - n_lat
    cblk = n_lat // n_ctx
    y_spec = pl.BlockSpec((n_ctx, GROUP_W), lambda i: (cblk, 0))
    any_spec = pl.BlockSpec(memory_space=pl.ANY)
    return pl.pallas_call(
        functools.partial(_ctx_kernel, lam_init=lam_init),
        grid=(1,),
        in_specs=[pl.BlockSpec(memory_space=pltpu.SMEM),
                  pl.BlockSpec((4, DIFF_QK_DIM), lambda i: (0, 0)),
                  pl.BlockSpec((1, GROUP_W), lambda i: (0, 0)),
                  pl.BlockSpec((n_ctx, N_GROUPS * GROUP_W), lambda i: (cblk, 0)),
                  any_spec, any_spec, any_spec, any_spec],
        out_specs=[y_spec] * 4,
        out_shape=[jax.ShapeDtypeStruct((n, GROUP_W), BF16)] * 4,
        input_output_aliases={4: 0, 5: 1, 6: 2, 7: 3},
        compiler_params=_params(("arbitrary",)),
        name="ctx_attn",
    )(sink, lam_p, subw, qkv, *ys)


def _merge_kernel(x_ref, mod_ref, nw_ref, y0, y1, y2, y3, wg_ref, bg_ref, wb_ref, wo_ref, o_ref, *, n_lat, tm):
    i = pl.program_id(0)
    row = i * tm + lax.broadcasted_iota(jnp.int32, (tm, 1), 0)
    is_ctx = row >= n_lat
    x = x_ref[...]
    h = _norm_modulate(x, nw_ref[...], _mod_rows(mod_ref, 0, is_ctx), _mod_rows(mod_ref, 1, is_ctx))
    hb = h.astype(BF16)
    acc = jnp.zeros((tm, D_MODEL), F32)
    for n, y_ref in enumerate((y0, y1, y2, y3)):
        cols = slice(n * D_MODEL, (n + 1) * D_MODEL)
        gate = jax.nn.sigmoid(_dot(hb, wg_ref[:, cols]) + bg_ref[:, cols])
        acc = acc + gate * _dot(y_ref[...], wb_ref[n])
    out = _dot(acc.astype(BF16), wo_ref[...])
    o_ref[...] = x + _mod_rows(mod_ref, 2, is_ctx) * out


def _merge(xs, mod, nw, ys, w_gate, b_gate, w_branch, w_out, layer, n_lat, n_rows):
    tm = _row_tile(n_rows)
    row_spec = lambda w: pl.BlockSpec((tm, w), lambda i: (i, 0))
    return pl.pallas_call(
        functools.partial(_merge_kernel, n_lat=n_lat, tm=tm),
        grid=(n_rows // tm,),
        in_specs=[row_spec(D_MODEL),
                  _const_spec((8, 6 * D_MODEL), lambda i: (0, 0)),
                  _const_spec((None, 1, D_MODEL), lambda i: (layer, 0, 0)),
                  row_spec(GROUP_W), row_spec(GROUP_W), row_spec(GROUP_W), row_spec(GROUP_W),
                  _const_spec((None, D_MODEL, N_BRANCH * D_MODEL), lambda i: (layer, 0, 0)),
                  _const_spec((None, 1, N_BRANCH * D_MODEL), lambda i: (layer, 0, 0)),
                  _const_spec((None, N_BRANCH, GROUP_W, D_MODEL), lambda i: (layer, 0, 0, 0)),
                  _const_spec((None, D_MODEL, D_MODEL), lambda i: (layer, 0, 0))],
        out_specs=row_spec(D_MODEL),
        out_shape=jax.ShapeDtypeStruct((n_rows, D_MODEL), F32),
        compiler_params=_params(("parallel",)),
        name="merge",
    )(xs, mod, nw, *ys, w_gate, b_gate, w_branch, w_out)


def _top_desc(ref, count):
    vals = []
    for r in range(count):
        cur = ref[...]
        m = jnp.max(cur, axis=0, keepdims=True)
        vals.append(m)
        if r + 1 < count:
            ref[...] = jnp.where(cur == m, -jnp.inf, cur)
    return vals


_PAIR_RANKS = [(a, b) for a in range(PEER_TOPK) for b in range(PEER_TOPK) if (a + 1) * (b + 1) <= PEER_TOPK]
_N_CAND = -(-len(_PAIR_RANKS) // 8) * 8


def _peer_kernel(x_ref, mod_ref, nw_ref, wqt_ref, keys_ref, u_ref, vt_ref, fw_ref, o_ref,
                 h2_s, s1_s, s2_s, e1_s, e2_s, tau_s, work_s, cand_s, p_s, acc_s,
                 *, n_lat, tm, te, final_norm):
    t = pl.program_id(0)
    e = pl.program_id(1)
    row = t * tm + lax.broadcasted_iota(jnp.int32, (tm, 1), 0)
    is_ctx = row >= n_lat

    @pl.when(e == 0)
    def _prepare():
        h2 = _norm_modulate(x_ref[...], nw_ref[...], _mod_rows(mod_ref, 3, is_ctx), _mod_rows(mod_ref, 4, is_ctx))
        h2_s[...] = h2.astype(BF16)
        hb = h2_s[...]
        cand_s[...] = jnp.full(cand_s.shape, -jnp.inf, F32)
        for hh in range(PEER_HEADS):
            tops = []
            for p, s_s in enumerate((s1_s, s2_s)):
                idx = 2 * hh + p
                qt = _dot_nt(wqt_ref[idx * PEER_KEY_DIM:(idx + 1) * PEER_KEY_DIM, :], hb)
                s_s[hh] = _dot(keys_ref[p, hh], qt.astype(BF16))
                work_s[...] = s_s[hh]
                tops.append(_top_desc(work_s, PEER_TOPK))
            for c, (ra, rb) in enumerate(_PAIR_RANKS):
                cand_s[c:c + 1, :] = tops[0][ra] + tops[1][rb]
            best = _top_desc(cand_s, PEER_TOPK)
            cand_s[...] = jnp.full(cand_s.shape, -jnp.inf, F32)
            z = best[0] * 0.0
            for bk in best:
                z = z + jnp.exp(bk - best[0])
            tau_s[hh] = best[PEER_TOPK - 1]
            e1_s[hh] = jnp.exp(s1_s[hh] - tops[0][0]) / z
            e2_s[hh] = jnp.exp(s2_s[hh] - tops[1][0])
        acc_s[...] = jnp.zeros(acc_s.shape, F32)

    a = _dot_nt(u_ref[...], h2_s[...])
    for ii in range(te // N_KEYS):
        i = e * (te // N_KEYS) + ii
        w = jnp.zeros((N_KEYS, tm), F32)
        for hh in range(PEER_HEADS):
            s_sum = s1_s[hh, pl.ds(i, 1), :] + s2_s[hh]
            w = w + jnp.where(s_sum >= tau_s[hh], e2_s[hh], 0.0) * e1_s[hh, pl.ds(i, 1), :]
        act = jax.nn.gelu(a[ii * N_KEYS:(ii + 1) * N_KEYS, :])
        p_s[ii * N_KEYS:(ii + 1) * N_KEYS, :] = (w * act).astype(BF16)
    acc_s[...] += _dot(vt_ref[...], p_s[...])

    @pl.when(e == pl.num_programs(1) - 1)
    def _finish():
        xn = x_ref[...] + _mod_rows(mod_ref, 5, is_ctx) * acc_s[...].T
        if final_norm:
            ms = jnp.mean(xn * xn, axis=-1, keepdims=True)
            xn = (xn * lax.rsqrt(ms + EPS)) * fw_ref[...]
        o_ref[...] = xn


def _peer(xs, mod, nw, wq_t, keys, u, v_t, final_w, layer, n_lat, n_rows, final_norm):
    tm = _row_tile(n_rows)
    te = 512
    n_exp = u.shape[1]
    hs = (PEER_HEADS, N_KEYS, tm)
    return pl.pallas_call(
        functools.partial(_peer_kernel, n_lat=n_lat, tm=tm, te=te, final_norm=final_norm),
        grid=(n_rows // tm, n_exp // te),
        in_specs=[pl.BlockSpec((tm, D_MODEL), lambda t, e: (t, 0)),
                  _const_spec((8, 6 * D_MODEL), lambda t, e: (0, 0)),
                  _const_spec((None, 1, D_MODEL), lambda t, e: (layer, 0, 0)),
                  _const_spec((None, 2 * PEER_HEADS * PEER_KEY_DIM, D_MODEL), lambda t, e: (layer, 0, 0)),
                  _const_spec((None, 2, PEER_HEADS, N_KEYS, PEER_KEY_DIM), lambda t, e: (layer, 0, 0, 0, 0)),
                  pl.BlockSpec((None, te, D_MODEL), lambda t, e: (layer, e, 0)),
                  pl.BlockSpec((None, D_MODEL, te), lambda t, e: (layer, 0, e)),
                  _const_spec((1, D_MODEL), lambda t, e: (0, 0))],
        out_specs=pl.BlockSpec((tm, D_MODEL), lambda t, e: (t, 0)),
        out_shape=jax.ShapeDtypeStruct((n_rows, D_MODEL), F32),
        scratch_shapes=[pltpu.VMEM((tm, D_MODEL), BF16),
                        pltpu.VMEM(hs, F32), pltpu.VMEM(hs, F32), pltpu.VMEM(hs, F32), pltpu.VMEM(hs, F32),
                        pltpu.VMEM((PEER_HEADS, 1, tm), F32),
                        pltpu.VMEM((N_KEYS, tm), F32),
                        pltpu.VMEM((_N_CAND, tm), F32),
                        pltpu.VMEM((te, tm), BF16),
                        pltpu.VMEM((D_MODEL, tm), F32)],
        compiler_params=_params(("parallel", "arbitrary")),
        name="peer",
    )(xs, mod, nw, wq_t, keys, u, v_t, final_w)


def _row_tile(n):
    for tm in (640, 512, 256):
        if n % tm == 0:
            return tm
    raise ValueError(f"unsupported row count {n}")


def _rope_tables(n_lat, n_ctx, dim):
    t = jnp.arange(n_lat, dtype=jnp.int32)
    row = (t // GRID_W).astype(F32)
    col = (t % GRID_W).astype(F32)
    n_freq = dim // 4
    inv_freq = ROPE_THETA ** (-jnp.arange(n_freq, dtype=F32) / n_freq)
    ang_r = row[:, None] * inv_freq[None, :]
    ang_c = col[:, None] * inv_freq[None, :]
    ang = jnp.concatenate([ang_r, ang_r, ang_c, ang_c], axis=-1)
    sign = np.where((np.arange(dim) % (dim // 2)) < dim // 4, -1.0, 1.0).astype(np.float32)
    cos = jnp.concatenate([jnp.cos(ang), jnp.ones((n_ctx, dim), F32)], axis=0)
    sin = jnp.concatenate([jnp.sin(ang) * sign, jnp.zeros((n_ctx, dim), F32)], axis=0)
    reps = GROUP_W // dim
    return jnp.tile(cos, (1, reps)), jnp.tile(sin, (1, reps))


def kernel(x, c, ctx, c_ctx, norm1_w, norm2_w, ada_w, ada_b, w_in, na_rpb, diff_lam, diff_subln_w, swa_sink,
           gqa_qk_norm_w, w_branch, w_gate, b_gate, w_out, peer_wq, peer_keys, peer_u, peer_v, final_norm_w):
    batch, n_lat, d = x.shape
    n_ctx = ctx.shape[1]
    depth = ada_w.shape[0]
    assert batch == 1 and d == D_MODEL and n_ctx == 256 and n_lat % 512 == 0 and n_lat >= NA_KBLKS * QBLK
    n = n_lat + n_ctx

    xs = jnp.concatenate([x[0], ctx[0]], axis=0)
    c2 = jnp.zeros((8, d), F32).at[0].set(c[0]).at[1].set(c_ctx)
    tabs = _rope_tables(n_lat, n_ctx, D_HEAD) + _rope_tables(n_lat, n_ctx, DIFF_QK_DIM)
    nw1 = norm1_w.reshape(depth, 1, d)
    nw2 = norm2_w.reshape(depth, 1, d)
    qkw = jnp.tile(gqa_qk_norm_w, (1, 1, GROUP_W // D_HEAD))
    subw = jnp.tile(diff_subln_w, (1, GROUP_W // D_HEAD))
    w_in_b = w_in.astype(BF16)
    w_gate_b = w_gate.astype(BF16)
    w_branch_b = w_branch.astype(BF16)
    w_out_b = w_out.astype(BF16)
    b_gate3 = b_gate.reshape(depth, 1, -1)
    wq_t = jnp.swapaxes(peer_wq, 1, 2).astype(BF16)
    keys_b = peer_keys.astype(BF16)
    u_b = peer_u.astype(BF16)
    v_t = jnp.swapaxes(peer_v, 1, 2).astype(BF16)
    final_w = final_norm_w.reshape(1, d)

    for l in range(depth):
        last = l == depth - 1
        lam_init = 0.8 - 0.6 * math.exp(-0.3 * l)
        n_rows = n_lat if last else n
        mod = _modulation(c2, ada_w, ada_b, l)
        qkv = _projection(xs, mod, nw1, w_in_b, tabs, qkw, l, n_lat)
        ys = (_na_attention(qkv, _na_bias(na_rpb[l], n_lat), n_lat),
              _diff_attention(qkv, diff_lam[l], subw[l:l + 1], lam_init, n_lat),
              _swa_attention(qkv, swa_sink[l], n_lat),
              _gqa_attention(qkv, n_lat))
        if not last:
            ys = _ctx_attention(qkv, ys, swa_sink[l], diff_lam[l], subw[l:l + 1], lam_init, n_lat)
        xs = _merge(xs, mod, nw1, ys, w_gate_b, b_gate3, w_branch_b, w_out_b, l, n_lat, n_rows)
        xs = _peer(xs, mod, nw2, wq_t, keys_b, u_b, v_t, final_w, l, n_lat, n_rows, last)
    return xs[:n_lat].reshape(batch, n_lat, d)
```

```python
import functools
import math

import numpy as np
import jax
import jax.numpy as jnp
from jax import lax
from jax.experimental import pallas as pl
from jax.experimental.pallas import tpu as pltpu

F32 = jnp.float32
BF16 = jnp.bfloat16

D_MODEL = 1024
GRID_W = 64
D_HEAD = 64
ROPE_THETA = 10000.0
EPS = 1e-6
NEG_INF = -1e30
LOG2E = math.log2(math.e)
NA_KH = 8
NA_KW = 16
DIFF_QK_DIM = 32
SWA_WINDOW = 128
N_BRANCH = 4
GROUP_W = 256
N_GROUPS = 10
PEER_HEADS = 8
N_KEYS = 128
PEER_KEY_DIM = 128
PEER_TOPK = 16
QBLK = 128
NA_KBLKS = 5
VMEM_LIMIT = 56 * 1024 * 1024


def _dot(a, b):
    return jnp.dot(a, b, preferred_element_type=F32)


def _dot_nt(a, b):
    return lax.dot_general(a, b, (((1,), (1,)), ((), ())), preferred_element_type=F32)


def _params(sem, vmem=VMEM_LIMIT):
    return pltpu.CompilerParams(dimension_semantics=sem, vmem_limit_bytes=vmem)


def _const_spec(shape, index_map):
    return pl.BlockSpec(shape, index_map, pipeline_mode=pl.Buffered(1))


def _mod_rows(mod_ref, k, is_ctx):
    lo, hi = k * D_MODEL, (k + 1) * D_MODEL
    return jnp.where(is_ctx, mod_ref[1:2, lo:hi], mod_ref[0:1, lo:hi])


def _norm_modulate(x, nw, shift, scale):
    ms = jnp.mean(x * x, axis=-1, keepdims=True)
    h = (x * lax.rsqrt(ms + EPS)) * nw
    return h * (1.0 + scale) + shift


def _group_mean_sq(y, width):
    w = y.shape[1]
    y2 = y * y
    r = lax.broadcasted_iota(jnp.int32, (w, w), 0) // width
    c = lax.broadcasted_iota(jnp.int32, (w, w), 1) // width
    ones = jnp.where(r == c, 1.0, 0.0).astype(BF16)
    hi = y2.astype(BF16)
    lo = (y2 - hi.astype(F32)).astype(BF16)
    return (_dot(hi, ones) + _dot(lo, ones)) * (1.0 / width)


def _rope(y, cos, sin_signed, quarter):
    w = y.shape[1]
    lane = lax.broadcasted_iota(jnp.int32, y.shape, 1)
    first = (lane % (2 * quarter)) < quarter
    rot = jnp.where(first, pltpu.roll(y, w - quarter, 1), pltpu.roll(y, quarter, 1))
    return y * cos + rot * sin_signed


def _softmax_pv(s, v, sink=None):
    m = jnp.max(s, axis=1, keepdims=True)
    if sink is not None:
        m = jnp.maximum(m, sink)
    p = jnp.exp(s - m)
    l = jnp.sum(p, axis=1, keepdims=True)
    if sink is not None:
        l = l + jnp.exp(sink - m)
    return _dot(p.astype(BF16), v) / l


def _diff_lambda(lam_ref, lam_init):
    lf = lam_ref[...]
    a = jnp.sum(lf[0:1] * lf[1:2], axis=1, keepdims=True)
    b = jnp.sum(lf[2:3] * lf[3:4], axis=1, keepdims=True)
    return jnp.exp(a) - jnp.exp(b) + lam_init


def _diff_finish(o, subw, lam_init):
    ms = _group_mean_sq(o, D_HEAD)
    return (o * lax.rsqrt(ms + EPS)) * subw * (1.0 - lam_init)


def _mod_kernel(c_ref, w_ref, b_ref, o_ref):
    c = c_ref[...]
    s = c * jax.nn.sigmoid(c)
    o_ref[...] = jnp.dot(s, w_ref[...], precision=lax.Precision.HIGHEST,
                         preferred_element_type=F32) + b_ref[...]


def _modulation(c2, ada_w, ada_b, layer):
    n_out = ada_w.shape[2]
    tn = 1536
    return pl.pallas_call(
        _mod_kernel,
        grid=(n_out // tn,),
        in_specs=[pl.BlockSpec((8, D_MODEL), lambda j: (0, 0)),
                  pl.BlockSpec((None, D_MODEL, tn), lambda j: (layer, 0, j)),
                  pl.BlockSpec((None, 1, tn), lambda j: (layer, 0, j))],
        out_specs=pl.BlockSpec((8, tn), lambda j: (0, j)),
        out_shape=jax.ShapeDtypeStruct((8, n_out), F32),
        compiler_params=_params(("arbitrary",)),
        name="adaln_mod",
    )(c2, ada_w, ada_b.reshape(ada_b.shape[0], 1, n_out))


def _proj_kernel(x_ref, mod_ref, nw_ref, w_ref, c64_ref, s64_ref, c32_ref, s32_ref, qkw_ref,
                 o_ref, dqt_ref, dvt_ref, gqt_ref, gvt_ref, *, n_lat, tm):
    i = pl.program_id(0)
    row = i * tm + lax.broadcasted_iota(jnp.int32, (tm, 1), 0)
    is_ctx = row >= n_lat
    h = _norm_modulate(x_ref[...], nw_ref[...], _mod_rows(mod_ref, 0, is_ctx), _mod_rows(mod_ref, 1, is_ctx))
    hb = h.astype(BF16)
    c64, s64 = c64_ref[...], s64_ref[...]
    c32, s32 = c32_ref[...], s32_ref[...]
    half = GROUP_W // 2
    sc64 = D_HEAD ** -0.5
    sc32 = DIFF_QK_DIM ** -0.5

    def qk_norm(y, w):
        return (y * lax.rsqrt(_group_mean_sq(y, D_HEAD) + EPS)) * w

    for g in range(N_GROUPS):
        y = _dot(hb, w_ref[:, g * GROUP_W:(g + 1) * GROUP_W])
        if g == 0:
            y = y * sc64
        elif g == 3:
            y = _rope(y, c32, s32, DIFF_QK_DIM // 4) * sc32
            dqt_ref[...] = (y * LOG2E).T.astype(BF16)
        elif g == 5:
            dvt_ref[...] = y.T.astype(BF16)
        elif g == 4:
            y = _rope(y, c32, s32, DIFF_QK_DIM // 4)
        elif g == 6:
            y = _rope(y, c64, s64, D_HEAD // 4) * sc64
        elif g == 7:
            k = _rope(y[:, :half], c64[:, :half], s64[:, :half], D_HEAD // 4)
            y = jnp.concatenate([k, y[:, half:]], axis=1)
        elif g == 8:
            y = _rope(qk_norm(y, qkw_ref[0:1, :]), c64, s64, D_HEAD // 4) * sc64
            gqt_ref[...] = (y * LOG2E).T.astype(BF16)
        elif g == 9:
            k = _rope(qk_norm(y[:, :half], qkw_ref[1:2, :half]), c64[:, :half], s64[:, :half], D_HEAD // 4)
            gvt_ref[...] = y[:, half:].T.astype(BF16)
            y = jnp.concatenate([k, y[:, half:]], axis=1)
        o_ref[:, g * GROUP_W:(g + 1) * GROUP_W] = y.astype(BF16)


def _projection(xs, mod, nw, w_in, tabs, qkw, layer, n_lat):
    n = xs.shape[0]
    tm = _row_tile(n)
    width = N_GROUPS * GROUP_W
    row_spec = lambda w: pl.BlockSpec((tm, w), lambda i: (i, 0))
    col_spec = lambda w: pl.BlockSpec((w, tm), lambda i: (0, i))
    t_shape = lambda w: jax.ShapeDtypeStruct((w, n), BF16)
    return pl.pallas_call(
        functools.partial(_proj_kernel, n_lat=n_lat, tm=tm),
        grid=(n // tm,),
        in_specs=[row_spec(D_MODEL),
                  _const_spec((8, 6 * D_MODEL), lambda i: (0, 0)),
                  _const_spec((None, 1, D_MODEL), lambda i: (layer, 0, 0)),
                  _const_spec((None, D_MODEL, width), lambda i: (layer, 0, 0)),
                  row_spec(GROUP_W), row_spec(GROUP_W), row_spec(GROUP_W), row_spec(GROUP_W),
                  _const_spec((None, 2, GROUP_W), lambda i: (layer, 0, 0))],
        out_specs=[row_spec(width), col_spec(GROUP_W), col_spec(GROUP_W), col_spec(GROUP_W),
                   col_spec(GROUP_W // 2)],
        out_shape=[jax.ShapeDtypeStruct((n, width), BF16), t_shape(GROUP_W), t_shape(GROUP_W), t_shape(GROUP_W),
                   t_shape(GROUP_W // 2)],
        compiler_params=_params(("parallel",)),
        name="proj",
    )(xs, mod, nw, w_in, *tabs, qkw)


def _na_kernel(q_ref, *refs):
    k_refs = refs[:NA_KBLKS]
    v_refs = refs[NA_KBLKS:2 * NA_KBLKS]
    kc_ref, vc_ref, bias_ref, o_ref = refs[2 * NA_KBLKS:]
    q = q_ref[...]
    ks = [r[...] for r in k_refs] + [kc_ref[...]]
    vs = [r[...] for r in v_refs] + [vc_ref[...]]
    outs = []
    for h in range(4):
        sl = slice(h * D_HEAD, (h + 1) * D_HEAD)
        qh = q[:, sl]
        s_loc = jnp.concatenate([_dot_nt(qh, k[:, sl]) for k in ks[:NA_KBLKS]], axis=1) + bias_ref[0, h]
        s = jnp.concatenate([s_loc, _dot_nt(qh, ks[NA_KBLKS][:, sl])], axis=1)
        v = jnp.concatenate([v[:, sl] for v in vs], axis=0)
        outs.append(_softmax_pv(s, v))
    o_ref[...] = jnp.concatenate(outs, axis=1).astype(BF16)


def _na_bias(rpb, n_lat):
    nb = n_lat // QBLK
    rows = n_lat // GRID_W
    rep = np.array([0, 1, 2, nb - 2, nb - 1])
    wstart = np.clip(rep - 2, 0, nb - NA_KBLKS)
    qi = np.arange(QBLK)
    kj = np.arange(NA_KBLKS * QBLK)
    qrow = 2 * rep[:, None] + qi[None, :] // GRID_W
    qcol = qi % GRID_W
    krow = 2 * wstart[:, None] + kj[None, :] // GRID_W
    kcol = kj % GRID_W
    rs = np.clip(qrow - NA_KH // 2, 0, rows - NA_KH)
    row_ok = (krow[:, None, :] >= rs[:, :, None]) & (krow[:, None, :] < rs[:, :, None] + NA_KH)
    cs = np.clip(qcol - NA_KW // 2, 0, GRID_W - NA_KW)
    col_ok = (kcol[None, :] >= cs[:, None]) & (kcol[None, :] < cs[:, None] + NA_KW)
    ok = row_ok & col_ok[None]
    gc = np.arange(GRID_W)
    cidx = np.clip(gc[None, :] - gc[:, None], -(NA_KW - 1), NA_KW - 1) + (NA_KW - 1)
    onehot = (cidx.reshape(-1)[None, :] == np.arange(2 * NA_KW - 1)[:, None]).astype(np.float32)
    table = jnp.einsum("hdc,cn->hdn", rpb.astype(F32), onehot, precision=lax.Precision.HIGHEST)
    table = table.reshape(rpb.shape[0], 2 * NA_KH - 1, GRID_W, GRID_W)
    zero = jnp.zeros((rpb.shape[0], GRID_W, GRID_W), F32)
    classes = []
    for c in range(len(rep)):
        q_rows = []
        for qr in range(QBLK // GRID_W):
            pieces = []
            for kr in range(NA_KBLKS * QBLK // GRID_W):
                d = int(2 * wstart[c] + kr - (2 * rep[c] + qr) + NA_KH - 1)
                pieces.append(table[:, d] if 0 <= d <= 2 * NA_KH - 2 else zero)
            q_rows.append(jnp.concatenate(pieces, axis=-1))
        classes.append(jnp.concatenate(q_rows, axis=-2))
    return jnp.where(ok[:, None], jnp.stack(classes), NEG_INF)


def _na_attention(qkv, bias, n_lat):
    n = qkv.shape[0]
    nb = n_lat // QBLK
    cblk = n_lat // 256

    def cls(b):
        return jnp.where(b < 2, b, jnp.where(b >= nb - 2, b - (nb - 5), 2))

    def kblk(j, col):
        return pl.BlockSpec((QBLK, GROUP_W), lambda b: (jnp.clip(b - 2, 0, nb - NA_KBLKS) + j, col))

    in_specs = ([pl.BlockSpec((QBLK, GROUP_W), lambda b: (b, 0))]
                + [kblk(j, 1) for j in range(NA_KBLKS)] + [kblk(j, 2) for j in range(NA_KBLKS)]
                + [pl.BlockSpec((256, GROUP_W), lambda b: (cblk, 1)),
                   pl.BlockSpec((256, GROUP_W), lambda b: (cblk, 2)),
                   pl.BlockSpec((1, 4, QBLK, NA_KBLKS * QBLK), lambda b: (cls(b), 0, 0, 0))])
    return pl.pallas_call(
        _na_kernel,
        grid=(nb,),
        in_specs=in_specs,
        out_specs=pl.BlockSpec((QBLK, GROUP_W), lambda b: (b, 0)),
        out_shape=jax.ShapeDtypeStruct((n, GROUP_W), BF16),
        compiler_params=_params(("parallel",)),
        name="na_attn",
    )(*([qkv] * (3 + 2 * NA_KBLKS)), bias)


def _swa_kernel(sink_ref, q_ref, km_ref, k0_ref, kp_ref, kvc_ref, o_ref, *, nb):
    b = pl.program_id(0)
    q = q_ref[...]
    blocks = [km_ref[...], k0_ref[...], kp_ref[...], kvc_ref[...]]
    row = lax.broadcasted_iota(jnp.int32, (QBLK, QBLK), 0)
    col = lax.broadcasted_iota(jnp.int32, (QBLK, QBLK), 1)
    ok_prev = jnp.logical_and(col >= row, b > 0)
    ok_next = jnp.logical_and(col <= row, b < nb - 1)
    outs = []
    for h in range(4):
        g = h // 2
        qh = q[:, h * D_HEAD:(h + 1) * D_HEAD]
        ksl = slice(g * D_HEAD, (g + 1) * D_HEAD)
        vsl = slice(GROUP_W // 2 + g * D_HEAD, GROUP_W // 2 + (g + 1) * D_HEAD)
        s = jnp.concatenate([jnp.where(ok_prev, _dot_nt(qh, blocks[0][:, ksl]), NEG_INF),
                             _dot_nt(qh, blocks[1][:, ksl]),
                             jnp.where(ok_next, _dot_nt(qh, blocks[2][:, ksl]), NEG_INF),
                             _dot_nt(qh, blocks[3][:, ksl])], axis=1)
        v = jnp.concatenate([blk[:, vsl] for blk in blocks], axis=0)
        outs.append(_softmax_pv(s, v, sink=sink_ref[h]))
    o_ref[...] = jnp.concatenate(outs, axis=1).astype(BF16)


def _swa_attention(qkv, sink, n_lat):
    n = qkv.shape[0]
    nb = n_lat // QBLK
    cblk = n_lat // 256
    return pl.pallas_call(
        functools.partial(_swa_kernel, nb=nb),
        grid=(nb,),
        in_specs=[pl.BlockSpec(memory_space=pltpu.SMEM),
                  pl.BlockSpec((QBLK, GROUP_W), lambda b: (b, 6)),
                  pl.BlockSpec((QBLK, GROUP_W), lambda b: (jnp.maximum(b - 1, 0), 7)),
                  pl.BlockSpec((QBLK, GROUP_W), lambda b: (b, 7)),
                  pl.BlockSpec((QBLK, GROUP_W), lambda b: (jnp.minimum(b + 1, nb - 1), 7)),
                  pl.BlockSpec((256, GROUP_W), lambda b: (cblk, 7))],
        out_specs=pl.BlockSpec((QBLK, GROUP_W), lambda b: (b, 0)),
        out_shape=jax.ShapeDtypeStruct((n, GROUP_W), BF16),
        compiler_params=_params(("parallel",)),
        name="swa_attn",
    )(sink, qkv, qkv, qkv, qkv, qkv)


ONES_ROWS = 16
ACC_ROWS = D_HEAD + ONES_ROWS


def _flash_units(units, qtu_ref, m_ref, acc_ref):
    n = len(units)
    st, pt, alpha = {}, {}, {}
    for t in range(n + 2):
        if t < n:
            u, k_lanes, _ = units[t]
            st[t] = _dot(k_lanes, qtu_ref[u])
        if 0 <= t - 1 < n:
            u = units[t - 1][0]
            s = st.pop(t - 1)
            m_old = m_ref[u]
            m_new = jnp.maximum(m_old, jnp.max(s, axis=0, keepdims=True))
            alpha[t - 1] = jnp.exp2(m_old - m_new)
            pt[t - 1] = jnp.exp2(s - m_new).astype(BF16)
            m_ref[u] = m_new
        if 0 <= t - 2 < n:
            u, _, vt_ext = units[t - 2]
            acc_ref[u] = alpha.pop(t - 2) * acc_ref[u] + _dot(vt_ext, pt.pop(t - 2))


def _flash_init(m_ref, acc_ref):
    m_ref[...] = jnp.full(m_ref.shape, NEG_INF, F32)
    acc_ref[...] = jnp.zeros(acc_ref.shape, F32)


def _with_ones(vt_h):
    return jnp.concatenate([vt_h, jnp.ones((ONES_ROWS, vt_h.shape[1]), BF16)], axis=0)


def _normalised(acc):
    return acc[:D_HEAD] / acc[D_HEAD:D_HEAD + 1]


def _diff_flash_kernel(lam_ref, subw_ref, qt_ref, k_ref, vt_ref, o_ref, qtu_ref, m_ref, acc_ref, *, lam_init):
    j = pl.program_id(1)
    lanes = 128

    @pl.when(j == 0)
    def _():
        _flash_init(m_ref, acc_ref)
        row = lax.broadcasted_iota(jnp.int32, (lanes, qt_ref.shape[1]), 0)
        for u in range(8):
            grp = qt_ref[(u // 4) * lanes:(u // 4 + 1) * lanes, :].astype(F32)
            lo = (u % 4) * DIFF_QK_DIM
            own = jnp.logical_and(row >= lo, row < lo + DIFF_QK_DIM)
            qtu_ref[u] = jnp.where(own, grp, 0.0).astype(BF16)

    k, vt = k_ref[...], vt_ref[...]
    vt_ext = [_with_ones(vt[h * D_HEAD:(h + 1) * D_HEAD]) for h in range(4)]
    _flash_units([(u, k[:, (u // 4) * lanes:(u // 4 + 1) * lanes], vt_ext[u // 2]) for u in range(8)],
                 qtu_ref, m_ref, acc_ref)

    @pl.when(j == pl.num_programs(1) - 1)
    def _():
        lam = _diff_lambda(lam_ref, lam_init)
        heads = []
        for h in range(4):
            o = _normalised(acc_ref[2 * h]) - lam * _normalised(acc_ref[2 * h + 1])
            heads.append(o * lax.rsqrt(jnp.mean(o * o, axis=0, keepdims=True) + EPS))
        ot = jnp.concatenate(heads, axis=0)
        o_ref[...] = (ot.T * subw_ref[...] * (1.0 - lam_init)).astype(BF16)


def _gqa_flash_kernel(qt_ref, k_ref, vt_ref, o_ref, qtu_ref, m_ref, acc_ref):
    j = pl.program_id(1)

    @pl.when(j == 0)
    def _():
        _flash_init(m_ref, acc_ref)
        zeros = jnp.zeros((D_HEAD, qt_ref.shape[1]), BF16)
        for h in range(4):
            qh = qt_ref[h * D_HEAD:(h + 1) * D_HEAD, :]
            qtu_ref[h] = jnp.concatenate([qh, zeros] if h < 2 else [zeros, qh], axis=0)

    k, vt = k_ref[...], vt_ref[...]
    vt_ext = [_with_ones(vt[g * D_HEAD:(g + 1) * D_HEAD]) for g in range(2)]
    _flash_units([(h, k, vt_ext[h // 2]) for h in range(4)], qtu_ref, m_ref, acc_ref)

    @pl.when(j == pl.num_programs(1) - 1)
    def _():
        ot = jnp.concatenate([_normalised(acc_ref[h]) for h in range(4)], axis=0)
        o_ref[...] = ot.T.astype(BF16)


def _flash_tiles(n, n_lat):
    tq = 512 if n_lat % 512 == 0 else 256
    tk = 1280 if n % 1280 == 0 else 256
    return tq, tk


def _flash_scratch(units, tq):
    return [pltpu.VMEM((units, 128, tq), BF16), pltpu.VMEM((units, 1, tq), F32),
            pltpu.VMEM((units, ACC_ROWS, tq), F32)]


def _diff_attention(qkv, qt, vt, lam_p, subw, lam_init, n_lat):
    n = qkv.shape[0]
    tq, tk = _flash_tiles(n, n_lat)
    return pl.pallas_call(
        functools.partial(_diff_flash_kernel, lam_init=lam_init),
        grid=(n_lat // tq, n // tk),
        in_specs=[pl.BlockSpec((4, DIFF_QK_DIM), lambda i, j: (0, 0)),
                  pl.BlockSpec((1, GROUP_W), lambda i, j: (0, 0)),
                  pl.BlockSpec((GROUP_W, tq), lambda i, j: (0, i)),
                  pl.BlockSpec((tk, GROUP_W), lambda i, j: (j, 4)),
                  pl.BlockSpec((GROUP_W, tk), lambda i, j: (0, j))],
        out_specs=pl.BlockSpec((tq, GROUP_W), lambda i, j: (i, 0)),
        out_shape=jax.ShapeDtypeStruct((n, GROUP_W), BF16),
        scratch_shapes=_flash_scratch(8, tq),
        compiler_params=_params(("parallel", "arbitrary")),
        name="diff_attn",
    )(lam_p, subw, qt, qkv, vt)


def _gqa_attention(qkv, qt, vt, n_lat):
    n = qkv.shape[0]
    tq, tk = _flash_tiles(n, n_lat)
    half = GROUP_W // 2
    return pl.pallas_call(
        _gqa_flash_kernel,
        grid=(n_lat // tq, n // tk),
        in_specs=[pl.BlockSpec((GROUP_W, tq), lambda i, j: (0, i)),
                  pl.BlockSpec((tk, half), lambda i, j: (j, 9 * GROUP_W // half)),
                  pl.BlockSpec((half, tk), lambda i, j: (0, j))],
        out_specs=pl.BlockSpec((tq, GROUP_W), lambda i, j: (i, 0)),
        out_shape=jax.ShapeDtypeStruct((n, GROUP_W), BF16),
        scratch_shapes=_flash_scratch(4, tq),
        compiler_params=_params(("parallel", "arbitrary")),
        name="gqa_attn",
    )(qt, qkv, vt)


def _ctx_kernel(sink_ref, lam_ref, subw_ref, qkv_ref, a0, a1, a2, a3, o_na, o_df, o_sw, o_gq, *, lam_init):
    del a0, a1, a2, a3
    grp = lambda g: qkv_ref[:, g * GROUP_W:(g + 1) * GROUP_W]
    hs = lambda h: slice(h * D_HEAD, (h + 1) * D_HEAD)
    half = GROUP_W // 2

    q, k, v = grp(0), grp(1), grp(2)
    o_na[...] = jnp.concatenate([_softmax_pv(_dot_nt(q[:, hs(h)], k[:, hs(h)]), v[:, hs(h)])
                                 for h in range(4)], axis=1).astype(BF16)

    q, k, v = grp(3), grp(4), grp(5)
    lam = _diff_lambda(lam_ref, lam_init)
    heads = []
    for h in range(4):
        maps = []
        for m in range(2):
            qs = slice((2 * h + m) * DIFF_QK_DIM, (2 * h + m + 1) * DIFF_QK_DIM)
            maps.append(_softmax_pv(_dot_nt(q[:, qs], k[:, qs]), v[:, hs(h)]))
        heads.append(maps[0] - lam * maps[1])
    o_df[...] = _diff_finish(jnp.concatenate(heads, axis=1), subw_ref[...], lam_init).astype(BF16)

    for q_g, kv_g, out, use_sink in ((6, 7, o_sw, True), (8, 9, o_gq, False)):
        q, kv = grp(q_g), grp(kv_g)
        outs = []
        for h in range(4):
            g = h // 2
            s = _dot_nt(q[:, hs(h)], kv[:, hs(g)])
            vh = kv[:, half + g * D_HEAD:half + (g + 1) * D_HEAD]
            outs.append(_softmax_pv(s, vh, sink=sink_ref[h] if use_sink else None))
        out[...] = jnp.concatenate(outs, axis=1).astype(BF16)


def _ctx_attention(qkv, ys, sink, lam_p, subw, lam_init, n_lat):
    n = qkv.shape[0]
    n_ctx = n - n_lat
    cblk = n_lat // n_ctx
    y_spec = pl.BlockSpec((n_ctx, GROUP_W), lambda i: (cblk, 0))
    any_spec = pl.BlockSpec(memory_space=pl.ANY)
    return pl.pallas_call(
        functools.partial(_ctx_kernel, lam_init=lam_init),
        grid=(1,),
        in_specs=[pl.BlockSpec(memory_space=pltpu.SMEM),
                  pl.BlockSpec((4, DIFF_QK_DIM), lambda i: (0, 0)),
                  pl.BlockSpec((1, GROUP_W), lambda i: (0, 0)),
                  pl.BlockSpec((n_ctx, N_GROUPS * GROUP_W), lambda i: (cblk, 0)),
                  any_spec, any_spec, any_spec, any_spec],
        out_specs=[y_spec] * 4,
        out_shape=[jax.ShapeDtypeStruct((n, GROUP_W), BF16)] * 4,
        input_output_aliases={4: 0, 5: 1, 6: 2, 7: 3},
        compiler_params=_params(("arbitrary",)),
        name="ctx_attn",
    )(sink, lam_p, subw, qkv, *ys)


def _merge_kernel(x_ref, mod_ref, nw_ref, y0, y1, y2, y3, wg_ref, bg_ref, wb_ref, wo_ref, o_ref, *, n_lat, tm):
    i = pl.program_id(0)
    row = i * tm + lax.broadcasted_iota(jnp.int32, (tm, 1), 0)
    is_ctx = row >= n_lat
    x = x_ref[...]
    h = _norm_modulate(x, nw_ref[...], _mod_rows(mod_ref, 0, is_ctx), _mod_rows(mod_ref, 1, is_ctx))
    hb = h.astype(BF16)
    acc = jnp.zeros((tm, D_MODEL), F32)
    for n, y_ref in enumerate((y0, y1, y2, y3)):
        cols = slice(n * D_MODEL, (n + 1) * D_MODEL)
        gate = jax.nn.sigmoid(_dot(hb, wg_ref[:, cols]) + bg_ref[:, cols])
        acc = acc + gate * _dot(y_ref[...], wb_ref[n])
    out = _dot(acc.astype(BF16), wo_ref[...])
    o_ref[...] = x + _mod_rows(mod_ref, 2, is_ctx) * out


def _merge(xs, mod, nw, ys, w_gate, b_gate, w_branch, w_out, layer, n_lat, n_rows):
    tm = _row_tile(n_rows)
    row_spec = lambda w: pl.BlockSpec((tm, w), lambda i: (i, 0))
    return pl.pallas_call(
        functools.partial(_merge_kernel, n_lat=n_lat, tm=tm),
        grid=(n_rows // tm,),
        in_specs=[row_spec(D_MODEL),
                  _const_spec((8, 6 * D_MODEL), lambda i: (0, 0)),
                  _const_spec((None, 1, D_MODEL), lambda i: (layer, 0, 0)),
                  row_spec(GROUP_W), row_spec(GROUP_W), row_spec(GROUP_W), row_spec(GROUP_W),
                  _const_spec((None, D_MODEL, N_BRANCH * D_MODEL), lambda i: (layer, 0, 0)),
                  _const_spec((None, 1, N_BRANCH * D_MODEL), lambda i: (layer, 0, 0)),
                  _const_spec((None, N_BRANCH, GROUP_W, D_MODEL), lambda i: (layer, 0, 0, 0)),
                  _const_spec((None, D_MODEL, D_MODEL), lambda i: (layer, 0, 0))],
        out_specs=row_spec(D_MODEL),
        out_shape=jax.ShapeDtypeStruct((n_rows, D_MODEL), F32),
        compiler_params=_params(("parallel",)),
        name="merge",
    )(xs, mod, nw, *ys, w_gate, b_gate, w_branch, w_out)


def _top_desc(ref, count):
    vals = []
    for r in range(count):
        cur = ref[...]
        m = jnp.max(cur, axis=0, keepdims=True)
        vals.append(m)
        if r + 1 < count:
            ref[...] = jnp.where(cur == m, -jnp.inf, cur)
    return vals


PEER_EXTRACT = PEER_TOPK + 1
_PAIR_RANKS = [(a, b) for a in range(PEER_EXTRACT) for b in range(PEER_EXTRACT) if (a + 1) * (b + 1) <= PEER_EXTRACT]
_N_CAND = -(-len(_PAIR_RANKS) // 8) * 8
PEER_TOKEN_CHUNK = 256
PEER_EXPERT_BLOCK = 256


def _peer_kernel(x_ref, mod_ref, nw_ref, wqt_ref, keys_ref, u_ref, vt_ref, fw_ref, base_ref, o_ref,
                 h2t_s, th_s, s2_s, e1_s, e2_s, work_s, cand_s, p_s, acc_s, *, grp, tm, te, final_norm):
    del base_ref
    e = pl.program_id(1)
    mod_row = lambda k: mod_ref[grp:grp + 1, k * D_MODEL:(k + 1) * D_MODEL]

    @pl.when(e == 0)
    def _prepare():
        h2 = _norm_modulate(x_ref[...], nw_ref[...], mod_row(3), mod_row(4))
        h2t_s[...] = h2.T.astype(BF16)
        hbt = h2t_s[...]
        cand_s[...] = jnp.full(cand_s.shape, -jnp.inf, F32)
        for hh in range(PEER_HEADS):
            tops, scores = [], []
            for p in range(2):
                idx = 2 * hh + p
                qt = _dot(wqt_ref[idx * PEER_KEY_DIM:(idx + 1) * PEER_KEY_DIM, :], hbt)
                s = _dot(keys_ref[p, hh], qt.astype(BF16))
                scores.append(s)
                work_s[...] = s
                tops.append(_top_desc(work_s, PEER_EXTRACT))
            for c, (ra, rb) in enumerate(_PAIR_RANKS):
                cand_s[c:c + 1, :] = tops[0][ra] + tops[1][rb]
            best = _top_desc(cand_s, PEER_EXTRACT)
            z = jnp.zeros_like(best[0])
            for bk in best[:PEER_TOPK]:
                z = z + jnp.exp(bk - best[0])
            tau = 0.5 * (best[PEER_TOPK - 1] + best[PEER_TOPK])
            th_s[hh] = tau - scores[0]
            s2_s[hh] = scores[1]
            e1_s[hh] = jnp.exp(scores[0] - tops[0][0]) / z
            e2_s[hh] = jnp.exp(scores[1] - tops[1][0])
        acc_s[...] = jnp.zeros(acc_s.shape, F32)

    n_blk = te // PEER_EXPERT_BLOCK
    per_blk = PEER_EXPERT_BLOCK // N_KEYS
    span = lambda b: slice(b * PEER_EXPERT_BLOCK, (b + 1) * PEER_EXPERT_BLOCK)
    acts = {0: _dot(u_ref[span(0), :], h2t_s[...])}
    total = None
    for b in range(n_blk):
        if b + 1 < n_blk:
            acts[b + 1] = _dot(u_ref[span(b + 1), :], h2t_s[...])
        a = acts.pop(b)
        for ii in range(per_blk):
            i = e * (te // N_KEYS) + b * per_blk + ii
            rows = slice(ii * N_KEYS, (ii + 1) * N_KEYS)
            prow = slice(b * PEER_EXPERT_BLOCK + ii * N_KEYS, b * PEER_EXPERT_BLOCK + (ii + 1) * N_KEYS)
            for tc in range(tm // PEER_TOKEN_CHUNK):
                cols = slice(tc * PEER_TOKEN_CHUNK, (tc + 1) * PEER_TOKEN_CHUNK)
                w = jnp.zeros((N_KEYS, PEER_TOKEN_CHUNK), F32)
                for hh in range(PEER_HEADS):
                    sel = s2_s[hh, :, cols] >= th_s[hh, pl.ds(i, 1), cols]
                    w = w + jnp.where(sel, e2_s[hh, :, cols], 0.0) * e1_s[hh, pl.ds(i, 1), cols]
                p_s[prow, cols] = (w * jax.nn.gelu(a[rows, cols])).astype(BF16)
        part = _dot(vt_ref[:, span(b)], p_s[span(b), :])
        total = part if total is None else total + part
    acc_s[...] += total

    @pl.when(e == pl.num_programs(1) - 1)
    def _finish():
        xn = x_ref[...] + mod_row(5) * acc_s[...].T
        if final_norm:
            ms = jnp.mean(xn * xn, axis=-1, keepdims=True)
            xn = (xn * lax.rsqrt(ms + EPS)) * fw_ref[...]
        o_ref[...] = xn


def _peer_call(xs, base, mod, nw, wq_t, keys, u, v_t, final_w, layer, grp, row0, n_rows, n_out, final_norm):
    tm = 512 if n_rows % 512 == 0 else 256
    te = 1024
    n_exp = u.shape[1]
    blk0 = row0 // tm
    hs = (PEER_HEADS, N_KEYS, tm)
    row_spec = pl.BlockSpec((tm, D_MODEL), lambda t, e: (blk0 + t, 0))
    return pl.pallas_call(
        functools.partial(_peer_kernel, grp=grp, tm=tm, te=te, final_norm=final_norm),
        grid=(n_rows // tm, n_exp // te),
        in_specs=[row_spec,
                  _const_spec((8, 6 * D_MODEL), lambda t, e: (0, 0)),
                  _const_spec((None, 1, D_MODEL), lambda t, e: (layer, 0, 0)),
                  _const_spec((None, 2 * PEER_HEADS * PEER_KEY_DIM, D_MODEL), lambda t, e: (layer, 0, 0)),
                  _const_spec((None, 2, PEER_HEADS, N_KEYS, PEER_KEY_DIM), lambda t, e: (layer, 0, 0, 0, 0)),
                  pl.BlockSpec((None, te, D_MODEL), lambda t, e: (layer, e, 0)),
                  pl.BlockSpec((None, D_MODEL, te), lambda t, e: (layer, 0, e)),
                  _const_spec((1, D_MODEL), lambda t, e: (0, 0)),
                  pl.BlockSpec(memory_space=pl.ANY)],
        out_specs=row_spec,
        out_shape=jax.ShapeDtypeStruct((n_out, D_MODEL), F32),
        input_output_aliases={} if base is xs else {8: 0},
        scratch_shapes=[pltpu.VMEM((D_MODEL, tm), BF16),
                        pltpu.VMEM(hs, F32), pltpu.VMEM(hs, F32), pltpu.VMEM(hs, F32), pltpu.VMEM(hs, F32),
                        pltpu.VMEM((N_KEYS, tm), F32),
                        pltpu.VMEM((_N_CAND, tm), F32),
                        pltpu.VMEM((te, tm), BF16),
                        pltpu.VMEM((D_MODEL, tm), F32)],
        compiler_params=_params(("parallel", "arbitrary")),
        name="peer",
    )(xs, mod, nw, wq_t, keys, u, v_t, final_w, base)


def _peer(xs, mod, nw, wq_t, keys, u, v_t, final_w, layer, n_lat, with_ctx, final_norm):
    n_out = xs.shape[0] if with_ctx else n_lat
    out = _peer_call(xs, xs, mod, nw, wq_t, keys, u, v_t, final_w, layer, 0, 0, n_lat, n_out, final_norm)
    if with_ctx:
        out = _peer_call(xs, out, mod, nw, wq_t, keys, u, v_t, final_w, layer, 1, n_lat, xs.shape[0] - n_lat,
                         n_out, final_norm)
    return out


def _row_tile(n):
    for tm in (640, 512, 256):
        if n % tm == 0:
            return tm
    raise ValueError(f"unsupported row count {n}")


def _rope_tables(n_lat, n_ctx, dim):
    t = jnp.arange(n_lat, dtype=jnp.int32)
    row = (t // GRID_W).astype(F32)
    col = (t % GRID_W).astype(F32)
    n_freq = dim // 4
    inv_freq = ROPE_THETA ** (-jnp.arange(n_freq, dtype=F32) / n_freq)
    ang_r = row[:, None] * inv_freq[None, :]
    ang_c = col[:, None] * inv_freq[None, :]
    ang = jnp.concatenate([ang_r, ang_r, ang_c, ang_c], axis=-1)
    sign = np.where((np.arange(dim) % (dim // 2)) < dim // 4, -1.0, 1.0).astype(np.float32)
    cos = jnp.concatenate([jnp.cos(ang), jnp.ones((n_ctx, dim), F32)], axis=0)
    sin = jnp.concatenate([jnp.sin(ang) * sign, jnp.zeros((n_ctx, dim), F32)], axis=0)
    reps = GROUP_W // dim
    return jnp.tile(cos, (1, reps)), jnp.tile(sin, (1, reps))


def kernel(x, c, ctx, c_ctx, norm1_w, norm2_w, ada_w, ada_b, w_in, na_rpb, diff_lam, diff_subln_w, swa_sink,
           gqa_qk_norm_w, w_branch, w_gate, b_gate, w_out, peer_wq, peer_keys, peer_u, peer_v, final_norm_w):
    batch, n_lat, d = x.shape
    n_ctx = ctx.shape[1]
    depth = ada_w.shape[0]
    assert batch == 1 and d == D_MODEL and n_ctx == 256 and n_lat % 512 == 0 and n_lat >= NA_KBLKS * QBLK
    n = n_lat + n_ctx

    xs = jnp.concatenate([x[0], ctx[0]], axis=0)
    c2 = jnp.zeros((8, d), F32).at[0].set(c[0]).at[1].set(c_ctx)
    tabs = _rope_tables(n_lat, n_ctx, D_HEAD) + _rope_tables(n_lat, n_ctx, DIFF_QK_DIM)
    nw1 = norm1_w.reshape(depth, 1, d)
    nw2 = norm2_w.reshape(depth, 1, d)
    qkw = jnp.tile(gqa_qk_norm_w, (1, 1, GROUP_W // D_HEAD))
    subw = jnp.tile(diff_subln_w, (1, GROUP_W // D_HEAD))
    w_in_b = w_in.astype(BF16)
    w_gate_b = w_gate.astype(BF16)
    w_branch_b = w_branch.astype(BF16)
    w_out_b = w_out.astype(BF16)
    b_gate3 = b_gate.reshape(depth, 1, -1)
    wq_t = jnp.swapaxes(peer_wq, 1, 2).astype(BF16)
    keys_b = peer_keys.astype(BF16)
    u_b = peer_u.astype(BF16)
    v_t = jnp.swapaxes(peer_v, 1, 2).astype(BF16)
    final_w = final_norm_w.reshape(1, d)

    for l in range(depth):
        last = l == depth - 1
        lam_init = 0.8 - 0.6 * math.exp(-0.3 * l)
        n_rows = n_lat if last else n
        mod = _modulation(c2, ada_w, ada_b, l)
        qkv, dqt, dvt, gqt, gvt = _projection(xs, mod, nw1, w_in_b, tabs, qkw, l, n_lat)
        ys = (_na_attention(qkv, _na_bias(na_rpb[l], n_lat), n_lat),
              _diff_attention(qkv, dqt, dvt, diff_lam[l], subw[l:l + 1], lam_init, n_lat),
              _swa_attention(qkv, swa_sink[l], n_lat),
              _gqa_attention(qkv, gqt, gvt, n_lat))
        if not last:
            ys = _ctx_attention(qkv, ys, swa_sink[l], diff_lam[l], subw[l:l + 1], lam_init, n_lat)
        xs = _merge(xs, mod, nw1, ys, w_gate_b, b_gate3, w_branch_b, w_out_b, l, n_lat, n_rows)
        xs = _peer(xs, mod, nw2, wq_t, keys_b, u_b, v_t, final_w, l, n_lat, not last, last)
    return xs[:n_lat].reshape(batch, n_lat, d)
```

```python
import functools
import math

import numpy as np
import jax
import jax.numpy as jnp
from jax import lax
from jax.experimental import pallas as pl
from jax.experimental.pallas import tpu as pltpu

F32 = jnp.float32
BF16 = jnp.bfloat16

D_MODEL = 1024
GRID_W = 64
D_HEAD = 64
ROPE_THETA = 10000.0
EPS = 1e-6
NEG_INF = -1e30
LOG2E = math.log2(math.e)
NA_KH = 8
NA_KW = 16
DIFF_QK_DIM = 32
SWA_WINDOW = 128
N_BRANCH = 4
GROUP_W = 256
N_GROUPS = 10
PEER_HEADS = 8
N_KEYS = 128
PEER_KEY_DIM = 128
PEER_TOPK = 16
QBLK = 128
NA_KBLKS = 5
VMEM_LIMIT = 56 * 1024 * 1024


def _dot(a, b):
    return jnp.dot(a, b, preferred_element_type=F32)


def _dot_nt(a, b):
    return lax.dot_general(a, b, (((1,), (1,)), ((), ())), preferred_element_type=F32)


def _params(sem, vmem=VMEM_LIMIT):
    return pltpu.CompilerParams(dimension_semantics=sem, vmem_limit_bytes=vmem)


def _const_spec(shape, index_map):
    return pl.BlockSpec(shape, index_map, pipeline_mode=pl.Buffered(1))


def _mod_rows(mod_ref, k, is_ctx):
    lo, hi = k * D_MODEL, (k + 1) * D_MODEL
    return jnp.where(is_ctx, mod_ref[1:2, lo:hi], mod_ref[0:1, lo:hi])


def _norm_modulate(x, nw, shift, scale):
    ms = jnp.mean(x * x, axis=-1, keepdims=True)
    h = (x * lax.rsqrt(ms + EPS)) * nw
    return h * (1.0 + scale) + shift


def _group_mean_sq(y, width):
    w = y.shape[1]
    y2 = y * y
    r = lax.broadcasted_iota(jnp.int32, (w, w), 0) // width
    c = lax.broadcasted_iota(jnp.int32, (w, w), 1) // width
    ones = jnp.where(r == c, 1.0, 0.0).astype(BF16)
    hi = y2.astype(BF16)
    lo = (y2 - hi.astype(F32)).astype(BF16)
    return (_dot(hi, ones) + _dot(lo, ones)) * (1.0 / width)


def _rope(y, cos, sin_signed, quarter):
    w = y.shape[1]
    lane = lax.broadcasted_iota(jnp.int32, y.shape, 1)
    first = (lane % (2 * quarter)) < quarter
    rot = jnp.where(first, pltpu.roll(y, w - quarter, 1), pltpu.roll(y, quarter, 1))
    return y * cos + rot * sin_signed


def _softmax_pv(s, v, sink=None):
    m = jnp.max(s, axis=1, keepdims=True)
    if sink is not None:
        m = jnp.maximum(m, sink)
    p = jnp.exp(s - m)
    l = jnp.sum(p, axis=1, keepdims=True)
    if sink is not None:
        l = l + jnp.exp(sink - m)
    return _dot(p.astype(BF16), v) / l


def _diff_lambda(lam_ref, lam_init):
    lf = lam_ref[...]
    a = jnp.sum(lf[0:1] * lf[1:2], axis=1, keepdims=True)
    b = jnp.sum(lf[2:3] * lf[3:4], axis=1, keepdims=True)
    return jnp.exp(a) - jnp.exp(b) + lam_init


def _diff_finish(o, subw, lam_init):
    ms = _group_mean_sq(o, D_HEAD)
    return (o * lax.rsqrt(ms + EPS)) * subw * (1.0 - lam_init)


def _mod_kernel(c_ref, w_ref, b_ref, o_ref):
    c = c_ref[...]
    s = c * jax.nn.sigmoid(c)
    o_ref[...] = jnp.dot(s, w_ref[...], precision=lax.Precision.HIGHEST,
                         preferred_element_type=F32) + b_ref[...]


def _modulation(c2, ada_w, ada_b, layer):
    n_out = ada_w.shape[2]
    tn = 1536
    return pl.pallas_call(
        _mod_kernel,
        grid=(n_out // tn,),
        in_specs=[pl.BlockSpec((8, D_MODEL), lambda j: (0, 0)),
                  pl.BlockSpec((None, D_MODEL, tn), lambda j: (layer, 0, j)),
                  pl.BlockSpec((None, 1, tn), lambda j: (layer, 0, j))],
        out_specs=pl.BlockSpec((8, tn), lambda j: (0, j)),
        out_shape=jax.ShapeDtypeStruct((8, n_out), F32),
        compiler_params=_params(("arbitrary",)),
        name="adaln_mod",
    )(c2, ada_w, ada_b.reshape(ada_b.shape[0], 1, n_out))


def _proj_kernel(x_ref, mod_ref, nw_ref, w_ref, c64_ref, s64_ref, c32_ref, s32_ref, qkw_ref,
                 o_ref, dqt_ref, dvt_ref, gqt_ref, gvt_ref, *, n_lat, tm):
    i = pl.program_id(0)
    row = i * tm + lax.broadcasted_iota(jnp.int32, (tm, 1), 0)
    is_ctx = row >= n_lat
    h = _norm_modulate(x_ref[...], nw_ref[...], _mod_rows(mod_ref, 0, is_ctx), _mod_rows(mod_ref, 1, is_ctx))
    hb = h.astype(BF16)
    c64, s64 = c64_ref[...], s64_ref[...]
    c32, s32 = c32_ref[...], s32_ref[...]
    half = GROUP_W // 2
    sc64 = D_HEAD ** -0.5
    sc32 = DIFF_QK_DIM ** -0.5

    def qk_norm(y, w):
        return (y * lax.rsqrt(_group_mean_sq(y, D_HEAD) + EPS)) * w

    for g in range(N_GROUPS):
        y = _dot(hb, w_ref[:, g * GROUP_W:(g + 1) * GROUP_W])
        if g == 0:
            y = y * sc64
        elif g == 3:
            y = _rope(y, c32, s32, DIFF_QK_DIM // 4) * sc32
            dqt_ref[...] = (y * LOG2E).T.astype(BF16)
        elif g == 5:
            dvt_ref[...] = y.T.astype(BF16)
        elif g == 4:
            y = _rope(y, c32, s32, DIFF_QK_DIM // 4)
        elif g == 6:
            y = _rope(y, c64, s64, D_HEAD // 4) * sc64
        elif g == 7:
            k = _rope(y[:, :half], c64[:, :half], s64[:, :half], D_HEAD // 4)
            y = jnp.concatenate([k, y[:, half:]], axis=1)
        elif g == 8:
            y = _rope(qk_norm(y, qkw_ref[0:1, :]), c64, s64, D_HEAD // 4) * sc64
            gqt_ref[...] = (y * LOG2E).T.astype(BF16)
        elif g == 9:
            k = _rope(qk_norm(y[:, :half], qkw_ref[1:2, :half]), c64[:, :half], s64[:, :half], D_HEAD // 4)
            gvt_ref[...] = y[:, half:].T.astype(BF16)
            y = jnp.concatenate([k, y[:, half:]], axis=1)
        o_ref[:, g * GROUP_W:(g + 1) * GROUP_W] = y.astype(BF16)


def _projection(xs, mod, nw, w_in, tabs, qkw, layer, n_lat):
    n = xs.shape[0]
    tm = _row_tile(n)
    width = N_GROUPS * GROUP_W
    row_spec = lambda w: pl.BlockSpec((tm, w), lambda i: (i, 0))
    col_spec = lambda w: pl.BlockSpec((w, tm), lambda i: (0, i))
    t_shape = lambda w: jax.ShapeDtypeStruct((w, n), BF16)
    return pl.pallas_call(
        functools.partial(_proj_kernel, n_lat=n_lat, tm=tm),
        grid=(n // tm,),
        in_specs=[row_spec(D_MODEL),
                  _const_spec((8, 6 * D_MODEL), lambda i: (0, 0)),
                  _const_spec((None, 1, D_MODEL), lambda i: (layer, 0, 0)),
                  _const_spec((None, D_MODEL, width), lambda i: (layer, 0, 0)),
                  row_spec(GROUP_W), row_spec(GROUP_W), row_spec(GROUP_W), row_spec(GROUP_W),
                  _const_spec((None, 2, GROUP_W), lambda i: (layer, 0, 0))],
        out_specs=[row_spec(width), col_spec(GROUP_W), col_spec(GROUP_W), col_spec(GROUP_W),
                   col_spec(GROUP_W // 2)],
        out_shape=[jax.ShapeDtypeStruct((n, width), BF16), t_shape(GROUP_W), t_shape(GROUP_W), t_shape(GROUP_W),
                   t_shape(GROUP_W // 2)],
        compiler_params=_params(("parallel",)),
        name="proj",
    )(xs, mod, nw, w_in, *tabs, qkw)


def _na_kernel(q_ref, *refs):
    k_refs = refs[:NA_KBLKS]
    v_refs = refs[NA_KBLKS:2 * NA_KBLKS]
    kc_ref, vc_ref, bias_ref, o_ref = refs[2 * NA_KBLKS:]
    q = q_ref[...]
    ks = [r[...] for r in k_refs] + [kc_ref[...]]
    vs = [r[...] for r in v_refs] + [vc_ref[...]]
    outs = []
    for h in range(4):
        sl = slice(h * D_HEAD, (h + 1) * D_HEAD)
        qh = q[:, sl]
        s_loc = jnp.concatenate([_dot_nt(qh, k[:, sl]) for k in ks[:NA_KBLKS]], axis=1) + bias_ref[0, h]
        s = jnp.concatenate([s_loc, _dot_nt(qh, ks[NA_KBLKS][:, sl])], axis=1)
        v = jnp.concatenate([v[:, sl] for v in vs], axis=0)
        outs.append(_softmax_pv(s, v))
    o_ref[...] = jnp.concatenate(outs, axis=1).astype(BF16)


def _na_bias(rpb, n_lat):
    nb = n_lat // QBLK
    rows = n_lat // GRID_W
    rep = np.array([0, 1, 2, nb - 2, nb - 1])
    wstart = np.clip(rep - 2, 0, nb - NA_KBLKS)
    qi = np.arange(QBLK)
    kj = np.arange(NA_KBLKS * QBLK)
    qrow = 2 * rep[:, None] + qi[None, :] // GRID_W
    qcol = qi % GRID_W
    krow = 2 * wstart[:, None] + kj[None, :] // GRID_W
    kcol = kj % GRID_W
    rs = np.clip(qrow - NA_KH // 2, 0, rows - NA_KH)
    row_ok = (krow[:, None, :] >= rs[:, :, None]) & (krow[:, None, :] < rs[:, :, None] + NA_KH)
    cs = np.clip(qcol - NA_KW // 2, 0, GRID_W - NA_KW)
    col_ok = (kcol[None, :] >= cs[:, None]) & (kcol[None, :] < cs[:, None] + NA_KW)
    ok = row_ok & col_ok[None]
    gc = np.arange(GRID_W)
    cidx = np.clip(gc[None, :] - gc[:, None], -(NA_KW - 1), NA_KW - 1) + (NA_KW - 1)
    onehot = (cidx.reshape(-1)[None, :] == np.arange(2 * NA_KW - 1)[:, None]).astype(np.float32)
    table = jnp.einsum("hdc,cn->hdn", rpb.astype(F32), onehot, precision=lax.Precision.HIGHEST)
    table = table.reshape(rpb.shape[0], 2 * NA_KH - 1, GRID_W, GRID_W)
    zero = jnp.zeros((rpb.shape[0], GRID_W, GRID_W), F32)
    classes = []
    for c in range(len(rep)):
        q_rows = []
        for qr in range(QBLK // GRID_W):
            pieces = []
            for kr in range(NA_KBLKS * QBLK // GRID_W):
                d = int(2 * wstart[c] + kr - (2 * rep[c] + qr) + NA_KH - 1)
                pieces.append(table[:, d] if 0 <= d <= 2 * NA_KH - 2 else zero)
            q_rows.append(jnp.concatenate(pieces, axis=-1))
        classes.append(jnp.concatenate(q_rows, axis=-2))
    return jnp.where(ok[:, None], jnp.stack(classes), NEG_INF)


def _na_attention(qkv, bias, n_lat):
    n = qkv.shape[0]
    nb = n_lat // QBLK
    cblk = n_lat // 256

    def cls(b):
        return jnp.where(b < 2, b, jnp.where(b >= nb - 2, b - (nb - 5), 2))

    def kblk(j, col):
        return pl.BlockSpec((QBLK, GROUP_W), lambda b: (jnp.clip(b - 2, 0, nb - NA_KBLKS) + j, col))

    in_specs = ([pl.BlockSpec((QBLK, GROUP_W), lambda b: (b, 0))]
                + [kblk(j, 1) for j in range(NA_KBLKS)] + [kblk(j, 2) for j in range(NA_KBLKS)]
                + [pl.BlockSpec((256, GROUP_W), lambda b: (cblk, 1)),
                   pl.BlockSpec((256, GROUP_W), lambda b: (cblk, 2)),
                   pl.BlockSpec((1, 4, QBLK, NA_KBLKS * QBLK), lambda b: (cls(b), 0, 0, 0))])
    return pl.pallas_call(
        _na_kernel,
        grid=(nb,),
        in_specs=in_specs,
        out_specs=pl.BlockSpec((QBLK, GROUP_W), lambda b: (b, 0)),
        out_shape=jax.ShapeDtypeStruct((n, GROUP_W), BF16),
        compiler_params=_params(("parallel",)),
        name="na_attn",
    )(*([qkv] * (3 + 2 * NA_KBLKS)), bias)


def _swa_kernel(sink_ref, q_ref, km_ref, k0_ref, kp_ref, kvc_ref, o_ref, *, nb):
    b = pl.program_id(0)
    q = q_ref[...]
    blocks = [km_ref[...], k0_ref[...], kp_ref[...], kvc_ref[...]]
    row = lax.broadcasted_iota(jnp.int32, (QBLK, QBLK), 0)
    col = lax.broadcasted_iota(jnp.int32, (QBLK, QBLK), 1)
    ok_prev = jnp.logical_and(col >= row, b > 0)
    ok_next = jnp.logical_and(col <= row, b < nb - 1)
    outs = []
    for h in range(4):
        g = h // 2
        qh = q[:, h * D_HEAD:(h + 1) * D_HEAD]
        ksl = slice(g * D_HEAD, (g + 1) * D_HEAD)
        vsl = slice(GROUP_W // 2 + g * D_HEAD, GROUP_W // 2 + (g + 1) * D_HEAD)
        s = jnp.concatenate([jnp.where(ok_prev, _dot_nt(qh, blocks[0][:, ksl]), NEG_INF),
                             _dot_nt(qh, blocks[1][:, ksl]),
                             jnp.where(ok_next, _dot_nt(qh, blocks[2][:, ksl]), NEG_INF),
                             _dot_nt(qh, blocks[3][:, ksl])], axis=1)
        v = jnp.concatenate([blk[:, vsl] for blk in blocks], axis=0)
        outs.append(_softmax_pv(s, v, sink=sink_ref[h]))
    o_ref[...] = jnp.concatenate(outs, axis=1).astype(BF16)


def _swa_attention(qkv, sink, n_lat):
    n = qkv.shape[0]
    nb = n_lat // QBLK
    cblk = n_lat // 256
    return pl.pallas_call(
        functools.partial(_swa_kernel, nb=nb),
        grid=(nb,),
        in_specs=[pl.BlockSpec(memory_space=pltpu.SMEM),
                  pl.BlockSpec((QBLK, GROUP_W), lambda b: (b, 6)),
                  pl.BlockSpec((QBLK, GROUP_W), lambda b: (jnp.maximum(b - 1, 0), 7)),
                  pl.BlockSpec((QBLK, GROUP_W), lambda b: (b, 7)),
                  pl.BlockSpec((QBLK, GROUP_W), lambda b: (jnp.minimum(b + 1, nb - 1), 7)),
                  pl.BlockSpec((256, GROUP_W), lambda b: (cblk, 7))],
        out_specs=pl.BlockSpec((QBLK, GROUP_W), lambda b: (b, 0)),
        out_shape=jax.ShapeDtypeStruct((n, GROUP_W), BF16),
        compiler_params=_params(("parallel",)),
        name="swa_attn",
    )(sink, qkv, qkv, qkv, qkv, qkv)


ONES_ROWS = 16
ACC_ROWS = D_HEAD + ONES_ROWS
FLASH_QUERY_CHUNK = 512


def _flash_units(units, qtu_ref, m_ref, acc_ref):
    tq = qtu_ref.shape[2]
    width = min(tq, FLASH_QUERY_CHUNK)
    units = [(u, k_lanes, vt_ext, slice(c * width, (c + 1) * width))
             for (u, k_lanes, vt_ext) in units for c in range(tq // width)]
    n = len(units)
    st, pt, alpha = {}, {}, {}
    for t in range(n + 2):
        if t < n:
            u, k_lanes, _, cols = units[t]
            st[t] = _dot(k_lanes, qtu_ref[u, :, cols])
        if 0 <= t - 1 < n:
            u, _, _, cols = units[t - 1]
            s = st.pop(t - 1)
            m_old = m_ref[u, :, cols]
            m_new = jnp.maximum(m_old, jnp.max(s, axis=0, keepdims=True))
            alpha[t - 1] = jnp.exp2(m_old - m_new)
            pt[t - 1] = jnp.exp2(s - m_new).astype(BF16)
            m_ref[u, :, cols] = m_new
        if 0 <= t - 2 < n:
            u, _, vt_ext, cols = units[t - 2]
            acc_ref[u, :, cols] = alpha.pop(t - 2) * acc_ref[u, :, cols] + _dot(vt_ext, pt.pop(t - 2))


def _flash_init(m_ref, acc_ref):
    m_ref[...] = jnp.full(m_ref.shape, NEG_INF, F32)
    acc_ref[...] = jnp.zeros(acc_ref.shape, F32)


def _with_ones(vt_h):
    return jnp.concatenate([vt_h, jnp.ones((ONES_ROWS, vt_h.shape[1]), BF16)], axis=0)


def _normalised(acc):
    return acc[:D_HEAD] / acc[D_HEAD:D_HEAD + 1]


def _diff_flash_kernel(lam_ref, subw_ref, qt_ref, k_ref, vt_ref, o_ref, qtu_ref, m_ref, acc_ref, *, lam_init):
    j = pl.program_id(1)
    lanes = 128

    @pl.when(j == 0)
    def _():
        _flash_init(m_ref, acc_ref)
        row = lax.broadcasted_iota(jnp.int32, (lanes, qt_ref.shape[1]), 0)
        for u in range(8):
            grp = qt_ref[(u // 4) * lanes:(u // 4 + 1) * lanes, :].astype(F32)
            lo = (u % 4) * DIFF_QK_DIM
            own = jnp.logical_and(row >= lo, row < lo + DIFF_QK_DIM)
            qtu_ref[u] = jnp.where(own, grp, 0.0).astype(BF16)

    k, vt = k_ref[...], vt_ref[...]
    vt_ext = [_with_ones(vt[h * D_HEAD:(h + 1) * D_HEAD]) for h in range(4)]
    _flash_units([(u, k[:, (u // 4) * lanes:(u // 4 + 1) * lanes], vt_ext[u // 2]) for u in range(8)],
                 qtu_ref, m_ref, acc_ref)

    @pl.when(j == pl.num_programs(1) - 1)
    def _():
        lam = _diff_lambda(lam_ref, lam_init)
        heads = []
        for h in range(4):
            o = _normalised(acc_ref[2 * h]) - lam * _normalised(acc_ref[2 * h + 1])
            heads.append(o * lax.rsqrt(jnp.mean(o * o, axis=0, keepdims=True) + EPS))
        ot = jnp.concatenate(heads, axis=0)
        o_ref[...] = (ot.T * subw_ref[...] * (1.0 - lam_init)).astype(BF16)


def _gqa_flash_kernel(qt_ref, k_ref, vt_ref, o_ref, qtu_ref, m_ref, acc_ref):
    j = pl.program_id(1)

    @pl.when(j == 0)
    def _():
        _flash_init(m_ref, acc_ref)
        zeros = jnp.zeros((D_HEAD, qt_ref.shape[1]), BF16)
        for h in range(4):
            qh = qt_ref[h * D_HEAD:(h + 1) * D_HEAD, :]
            qtu_ref[h] = jnp.concatenate([qh, zeros] if h < 2 else [zeros, qh], axis=0)

    k, vt = k_ref[...], vt_ref[...]
    vt_ext = [_with_ones(vt[g * D_HEAD:(g + 1) * D_HEAD]) for g in range(2)]
    _flash_units([(h, k, vt_ext[h // 2]) for h in range(4)], qtu_ref, m_ref, acc_ref)

    @pl.when(j == pl.num_programs(1) - 1)
    def _():
        ot = jnp.concatenate([_normalised(acc_ref[h]) for h in range(4)], axis=0)
        o_ref[...] = ot.T.astype(BF16)


def _flash_tiles(n, n_lat):
    tq = 512 if n_lat % 512 == 0 else 256
    tk = 1280 if n % 1280 == 0 else 256
    return tq, tk


def _flash_scratch(units, tq):
    return [pltpu.VMEM((units, 128, tq), BF16), pltpu.VMEM((units, 1, tq), F32),
            pltpu.VMEM((units, ACC_ROWS, tq), F32)]


def _diff_attention(qkv, qt, vt, lam_p, subw, lam_init, n_lat):
    n = qkv.shape[0]
    tq, tk = _flash_tiles(n, n_lat)
    return pl.pallas_call(
        functools.partial(_diff_flash_kernel, lam_init=lam_init),
        grid=(n_lat // tq, n // tk),
        in_specs=[pl.BlockSpec((4, DIFF_QK_DIM), lambda i, j: (0, 0)),
                  pl.BlockSpec((1, GROUP_W), lambda i, j: (0, 0)),
                  pl.BlockSpec((GROUP_W, tq), lambda i, j: (0, i)),
                  pl.BlockSpec((tk, GROUP_W), lambda i, j: (j, 4)),
                  pl.BlockSpec((GROUP_W, tk), lambda i, j: (0, j))],
        out_specs=pl.BlockSpec((tq, GROUP_W), lambda i, j: (i, 0)),
        out_shape=jax.ShapeDtypeStruct((n, GROUP_W), BF16),
        scratch_shapes=_flash_scratch(8, tq),
        compiler_params=_params(("parallel", "arbitrary")),
        name="diff_attn",
    )(lam_p, subw, qt, qkv, vt)


def _gqa_attention(qkv, qt, vt, n_lat):
    n = qkv.shape[0]
    tq, tk = _flash_tiles(n, n_lat)
    half = GROUP_W // 2
    return pl.pallas_call(
        _gqa_flash_kernel,
        grid=(n_lat // tq, n // tk),
        in_specs=[pl.BlockSpec((GROUP_W, tq), lambda i, j: (0, i)),
                  pl.BlockSpec((tk, half), lambda i, j: (j, 9 * GROUP_W // half)),
                  pl.BlockSpec((half, tk), lambda i, j: (0, j))],
        out_specs=pl.BlockSpec((tq, GROUP_W), lambda i, j: (i, 0)),
        out_shape=jax.ShapeDtypeStruct((n, GROUP_W), BF16),
        scratch_shapes=_flash_scratch(4, tq),
        compiler_params=_params(("parallel", "arbitrary")),
        name="gqa_attn",
    )(qt, qkv, vt)


def _ctx_kernel(sink_ref, lam_ref, subw_ref, qkv_ref, a0, a1, a2, a3, o_na, o_df, o_sw, o_gq, *, lam_init):
    del a0, a1, a2, a3
    grp = lambda g: qkv_ref[:, g * GROUP_W:(g + 1) * GROUP_W]
    hs = lambda h: slice(h * D_HEAD, (h + 1) * D_HEAD)
    half = GROUP_W // 2

    q, k, v = grp(0), grp(1), grp(2)
    o_na[...] = jnp.concatenate([_softmax_pv(_dot_nt(q[:, hs(h)], k[:, hs(h)]), v[:, hs(h)])
                                 for h in range(4)], axis=1).astype(BF16)

    q, k, v = grp(3), grp(4), grp(5)
    lam = _diff_lambda(lam_ref, lam_init)
    heads = []
    for h in range(4):
        maps = []
        for m in range(2):
            qs = slice((2 * h + m) * DIFF_QK_DIM, (2 * h + m + 1) * DIFF_QK_DIM)
            maps.append(_softmax_pv(_dot_nt(q[:, qs], k[:, qs]), v[:, hs(h)]))
        heads.append(maps[0] - lam * maps[1])
    o_df[...] = _diff_finish(jnp.concatenate(heads, axis=1), subw_ref[...], lam_init).astype(BF16)

    for q_g, kv_g, out, use_sink in ((6, 7, o_sw, True), (8, 9, o_gq, False)):
        q, kv = grp(q_g), grp(kv_g)
        outs = []
        for h in range(4):
            g = h // 2
            s = _dot_nt(q[:, hs(h)], kv[:, hs(g)])
            vh = kv[:, half + g * D_HEAD:half + (g + 1) * D_HEAD]
            outs.append(_softmax_pv(s, vh, sink=sink_ref[h] if use_sink else None))
        out[...] = jnp.concatenate(outs, axis=1).astype(BF16)


def _ctx_attention(qkv, ys, sink, lam_p, subw, lam_init, n_lat):
    n = qkv.shape[0]
    n_ctx = n - n_lat
    cblk = n_lat // n_ctx
    y_spec = pl.BlockSpec((n_ctx, GROUP_W), lambda i: (cblk, 0))
    any_spec = pl.BlockSpec(memory_space=pl.ANY)
    return pl.pallas_call(
        functools.partial(_ctx_kernel, lam_init=lam_init),
        grid=(1,),
        in_specs=[pl.BlockSpec(memory_space=pltpu.SMEM),
                  pl.BlockSpec((4, DIFF_QK_DIM), lambda i: (0, 0)),
                  pl.BlockSpec((1, GROUP_W), lambda i: (0, 0)),
                  pl.BlockSpec((n_ctx, N_GROUPS * GROUP_W), lambda i: (cblk, 0)),
                  any_spec, any_spec, any_spec, any_spec],
        out_specs=[y_spec] * 4,
        out_shape=[jax.ShapeDtypeStruct((n, GROUP_W), BF16)] * 4,
        input_output_aliases={4: 0, 5: 1, 6: 2, 7: 3},
        compiler_params=_params(("arbitrary",)),
        name="ctx_attn",
    )(sink, lam_p, subw, qkv, *ys)


def _merge_kernel(x_ref, mod_ref, nw_ref, y0, y1, y2, y3, wg_ref, bg_ref, wb_ref, wo_ref, o_ref, *, n_lat, tm):
    i = pl.program_id(0)
    row = i * tm + lax.broadcasted_iota(jnp.int32, (tm, 1), 0)
    is_ctx = row >= n_lat
    x = x_ref[...]
    h = _norm_modulate(x, nw_ref[...], _mod_rows(mod_ref, 0, is_ctx), _mod_rows(mod_ref, 1, is_ctx))
    hb = h.astype(BF16)
    acc = jnp.zeros((tm, D_MODEL), F32)
    for n, y_ref in enumerate((y0, y1, y2, y3)):
        cols = slice(n * D_MODEL, (n + 1) * D_MODEL)
        gate = jax.nn.sigmoid(_dot(hb, wg_ref[:, cols]) + bg_ref[:, cols])
        acc = acc + gate * _dot(y_ref[...], wb_ref[n])
    out = _dot(acc.astype(BF16), wo_ref[...])
    o_ref[...] = x + _mod_rows(mod_ref, 2, is_ctx) * out


def _merge(xs, mod, nw, ys, w_gate, b_gate, w_branch, w_out, layer, n_lat, n_rows):
    tm = _row_tile(n_rows)
    row_spec = lambda w: pl.BlockSpec((tm, w), lambda i: (i, 0))
    return pl.pallas_call(
        functools.partial(_merge_kernel, n_lat=n_lat, tm=tm),
        grid=(n_rows // tm,),
        in_specs=[row_spec(D_MODEL),
                  _const_spec((8, 6 * D_MODEL), lambda i: (0, 0)),
                  _const_spec((None, 1, D_MODEL), lambda i: (layer, 0, 0)),
                  row_spec(GROUP_W), row_spec(GROUP_W), row_spec(GROUP_W), row_spec(GROUP_W),
                  _const_spec((None, D_MODEL, N_BRANCH * D_MODEL), lambda i: (layer, 0, 0)),
                  _const_spec((None, 1, N_BRANCH * D_MODEL), lambda i: (layer, 0, 0)),
                  _const_spec((None, N_BRANCH, GROUP_W, D_MODEL), lambda i: (layer, 0, 0, 0)),
                  _const_spec((None, D_MODEL, D_MODEL), lambda i: (layer, 0, 0))],
        out_specs=row_spec(D_MODEL),
        out_shape=jax.ShapeDtypeStruct((n_rows, D_MODEL), F32),
        compiler_params=_params(("parallel",)),
        name="merge",
    )(xs, mod, nw, *ys, w_gate, b_gate, w_branch, w_out)


PEER_UNRANKED = 64.0


def _top_desc(ref, count, rank_ref=None):
    vals = []
    if rank_ref is not None:
        rank_ref[...] = jnp.full(rank_ref.shape, PEER_UNRANKED, F32)
    for r in range(count):
        cur = ref[...]
        m = jnp.max(cur, axis=0, keepdims=True)
        vals.append(m)
        hit = cur == m
        if rank_ref is not None:
            rank_ref[...] = jnp.where(hit, float(r), rank_ref[...])
        if r + 1 < count:
            ref[...] = jnp.where(hit, -jnp.inf, cur)
    return vals


def _gelu_tanh(x):
    y2 = (2.0 * math.sqrt(2.0 / math.pi)) * (x + 0.044715 * (x * x * x))
    return x / (1.0 + jnp.exp(-y2))


PEER_EXTRACT = PEER_TOPK + 1
_PAIR_RANKS = [(a, b) for a in range(PEER_EXTRACT) for b in range(PEER_EXTRACT) if (a + 1) * (b + 1) <= PEER_EXTRACT]
_N_CAND = -(-len(_PAIR_RANKS) // 8) * 8
PEER_TOKEN_CHUNK = 256
PEER_EXPERT_BLOCK = 256


def _peer_kernel(x_ref, mod_ref, nw_ref, wqt_ref, keys_ref, u_ref, vt_ref, fw_ref, base_ref, o_ref,
                 h2t_s, cnt_s, e1_s, rank2_s, e2_s, work_s, rank_s, cand_s, p_s, acc_s,
                 *, grp, tm, te, final_norm):
    del base_ref
    e = pl.program_id(1)
    mod_row = lambda k: mod_ref[grp:grp + 1, k * D_MODEL:(k + 1) * D_MODEL]

    @pl.when(e == 0)
    def _prepare():
        h2 = _norm_modulate(x_ref[...], nw_ref[...], mod_row(3), mod_row(4))
        h2t_s[...] = h2.T.astype(BF16)
        hbt = h2t_s[...]
        cand_s[...] = jnp.full(cand_s.shape, -jnp.inf, F32)
        for hh in range(PEER_HEADS):
            tops, scores = [], []
            for p in range(2):
                idx = 2 * hh + p
                qt = _dot(wqt_ref[idx * PEER_KEY_DIM:(idx + 1) * PEER_KEY_DIM, :], hbt)
                s = _dot(keys_ref[p, hh], qt.astype(BF16))
                scores.append(s)
                work_s[...] = s
                tops.append(_top_desc(work_s, PEER_EXTRACT, rank_s if p == 1 else None))
            for c, (ra, rb) in enumerate(_PAIR_RANKS):
                cand_s[c:c + 1, :] = tops[0][ra] + tops[1][rb]
            best = _top_desc(cand_s, PEER_EXTRACT)
            z = jnp.zeros_like(best[0])
            for bk in best[:PEER_TOPK]:
                z = z + jnp.exp(bk - best[0])
            tau = 0.5 * (best[PEER_TOPK - 1] + best[PEER_TOPK])
            need = tau - scores[0]
            cnt = jnp.zeros_like(need)
            for r, a2 in enumerate(tops[1]):
                cnt = jnp.where(a2 >= need, float(r + 1), cnt)
            cnt_s[hh] = cnt
            rank2_s[hh] = rank_s[...].astype(BF16)
            e1_s[hh] = jnp.exp(scores[0] - tops[0][0]) / z
            e2_s[hh] = jnp.exp(scores[1] - tops[1][0]).astype(BF16)
        acc_s[...] = jnp.zeros(acc_s.shape, F32)

    n_blk = te // PEER_EXPERT_BLOCK
    per_blk = PEER_EXPERT_BLOCK // N_KEYS
    span = lambda b: slice(b * PEER_EXPERT_BLOCK, (b + 1) * PEER_EXPERT_BLOCK)
    acts = {0: _dot(u_ref[span(0), :], h2t_s[...])}
    total = None
    for b in range(n_blk):
        if b + 1 < n_blk:
            acts[b + 1] = _dot(u_ref[span(b + 1), :], h2t_s[...])
        a = acts.pop(b)
        for ii in range(per_blk):
            i = e * (te // N_KEYS) + b * per_blk + ii
            rows = slice(ii * N_KEYS, (ii + 1) * N_KEYS)
            prow = slice(b * PEER_EXPERT_BLOCK + ii * N_KEYS, b * PEER_EXPERT_BLOCK + (ii + 1) * N_KEYS)
            for tc in range(tm // PEER_TOKEN_CHUNK):
                cols = slice(tc * PEER_TOKEN_CHUNK, (tc + 1) * PEER_TOKEN_CHUNK)
                w = jnp.zeros((N_KEYS, PEER_TOKEN_CHUNK), BF16)
                for hh in range(PEER_HEADS):
                    sel = rank2_s[hh, :, cols] < cnt_s[hh, pl.ds(i, 1), cols].astype(BF16)
                    gate = jnp.where(sel, e2_s[hh, :, cols], jnp.zeros((), BF16))
                    w = w + gate * e1_s[hh, pl.ds(i, 1), cols].astype(BF16)
                p_s[prow, cols] = w * _gelu_tanh(a[rows, cols]).astype(BF16)
        part = _dot(vt_ref[:, span(b)], p_s[span(b), :])
        total = part if total is None else total + part
    acc_s[...] += total

    @pl.when(e == pl.num_programs(1) - 1)
    def _finish():
        xn = x_ref[...] + mod_row(5) * acc_s[...].T
        if final_norm:
            ms = jnp.mean(xn * xn, axis=-1, keepdims=True)
            xn = (xn * lax.rsqrt(ms + EPS)) * fw_ref[...]
        o_ref[...] = xn


def _peer_call(xs, base, mod, nw, wq_t, keys, u, v_t, final_w, layer, grp, row0, n_rows, n_out, final_norm):
    tm = 512 if n_rows % 512 == 0 else 256
    te = 1024
    n_exp = u.shape[1]
    blk0 = row0 // tm
    hs = (PEER_HEADS, N_KEYS, tm)
    row_spec = pl.BlockSpec((tm, D_MODEL), lambda t, e: (blk0 + t, 0))
    return pl.pallas_call(
        functools.partial(_peer_kernel, grp=grp, tm=tm, te=te, final_norm=final_norm),
        grid=(n_rows // tm, n_exp // te),
        in_specs=[row_spec,
                  _const_spec((8, 6 * D_MODEL), lambda t, e: (0, 0)),
                  _const_spec((None, 1, D_MODEL), lambda t, e: (layer, 0, 0)),
                  _const_spec((None, 2 * PEER_HEADS * PEER_KEY_DIM, D_MODEL), lambda t, e: (layer, 0, 0)),
                  _const_spec((None, 2, PEER_HEADS, N_KEYS, PEER_KEY_DIM), lambda t, e: (layer, 0, 0, 0, 0)),
                  pl.BlockSpec((None, te, D_MODEL), lambda t, e: (layer, e, 0)),
                  pl.BlockSpec((None, D_MODEL, te), lambda t, e: (layer, 0, e)),
                  _const_spec((1, D_MODEL), lambda t, e: (0, 0)),
                  pl.BlockSpec(memory_space=pl.ANY)],
        out_specs=row_spec,
        out_shape=jax.ShapeDtypeStruct((n_out, D_MODEL), F32),
        input_output_aliases={} if base is xs else {8: 0},
        scratch_shapes=[pltpu.VMEM((D_MODEL, tm), BF16),
                        pltpu.VMEM(hs, F32), pltpu.VMEM(hs, F32), pltpu.VMEM(hs, BF16), pltpu.VMEM(hs, BF16),
                        pltpu.VMEM((N_KEYS, tm), F32), pltpu.VMEM((N_KEYS, tm), F32),
                        pltpu.VMEM((_N_CAND, tm), F32),
                        pltpu.VMEM((te, tm), BF16),
                        pltpu.VMEM((D_MODEL, tm), F32)],
        compiler_params=_params(("parallel", "arbitrary")),
        name="peer",
    )(xs, mod, nw, wq_t, keys, u, v_t, final_w, base)


def _peer(xs, mod, nw, wq_t, keys, u, v_t, final_w, layer, n_lat, with_ctx, final_norm):
    n_out = xs.shape[0] if with_ctx else n_lat
    out = _peer_call(xs, xs, mod, nw, wq_t, keys, u, v_t, final_w, layer, 0, 0, n_lat, n_out, final_norm)
    if with_ctx:
        out = _peer_call(xs, out, mod, nw, wq_t, keys, u, v_t, final_w, layer, 1, n_lat, xs.shape[0] - n_lat,
                         n_out, final_norm)
    return out


def _row_tile(n):
    for tm in (640, 512, 256):
        if n % tm == 0:
            return tm
    raise ValueError(f"unsupported row count {n}")


def _rope_tables(n_lat, n_ctx, dim):
    t = jnp.arange(n_lat, dtype=jnp.int32)
    row = (t // GRID_W).astype(F32)
    col = (t % GRID_W).astype(F32)
    n_freq = dim // 4
    inv_freq = ROPE_THETA ** (-jnp.arange(n_freq, dtype=F32) / n_freq)
    ang_r = row[:, None] * inv_freq[None, :]
    ang_c = col[:, None] * inv_freq[None, :]
    ang = jnp.concatenate([ang_r, ang_r, ang_c, ang_c], axis=-1)
    sign = np.where((np.arange(dim) % (dim // 2)) < dim // 4, -1.0, 1.0).astype(np.float32)
    cos = jnp.concatenate([jnp.cos(ang), jnp.ones((n_ctx, dim), F32)], axis=0)
    sin = jnp.concatenate([jnp.sin(ang) * sign, jnp.zeros((n_ctx, dim), F32)], axis=0)
    reps = GROUP_W // dim
    return jnp.tile(cos, (1, reps)), jnp.tile(sin, (1, reps))


def kernel(x, c, ctx, c_ctx, norm1_w, norm2_w, ada_w, ada_b, w_in, na_rpb, diff_lam, diff_subln_w, swa_sink,
           gqa_qk_norm_w, w_branch, w_gate, b_gate, w_out, peer_wq, peer_keys, peer_u, peer_v, final_norm_w):
    batch, n_lat, d = x.shape
    n_ctx = ctx.shape[1]
    depth = ada_w.shape[0]
    assert batch == 1 and d == D_MODEL and n_ctx == 256 and n_lat % 512 == 0 and n_lat >= NA_KBLKS * QBLK
    n = n_lat + n_ctx

    xs = jnp.concatenate([x[0], ctx[0]], axis=0)
    c2 = jnp.zeros((8, d), F32).at[0].set(c[0]).at[1].set(c_ctx)
    tabs = _rope_tables(n_lat, n_ctx, D_HEAD) + _rope_tables(n_lat, n_ctx, DIFF_QK_DIM)
    nw1 = norm1_w.reshape(depth, 1, d)
    nw2 = norm2_w.reshape(depth, 1, d)
    qkw = jnp.tile(gqa_qk_norm_w, (1, 1, GROUP_W // D_HEAD))
    subw = jnp.tile(diff_subln_w, (1, GROUP_W // D_HEAD))
    w_in_b = w_in.astype(BF16)
    w_gate_b = w_gate.astype(BF16)
    w_branch_b = w_branch.astype(BF16)
    w_out_b = w_out.astype(BF16)
    b_gate3 = b_gate.reshape(depth, 1, -1)
    wq_t = jnp.swapaxes(peer_wq, 1, 2).astype(BF16)
    keys_b = peer_keys.astype(BF16)
    u_b = peer_u.astype(BF16)
    v_t = jnp.swapaxes(peer_v, 1, 2).astype(BF16)
    final_w = final_norm_w.reshape(1, d)

    for l in range(depth):
        last = l == depth - 1
        lam_init = 0.8 - 0.6 * math.exp(-0.3 * l)
        n_rows = n_lat if last else n
        mod = _modulation(c2, ada_w, ada_b, l)
        qkv, dqt, dvt, gqt, gvt = _projection(xs, mod, nw1, w_in_b, tabs, qkw, l, n_lat)
        ys = (_na_attention(qkv, _na_bias(na_rpb[l], n_lat), n_lat),
              _diff_attention(qkv, dqt, dvt, diff_lam[l], subw[l:l + 1], lam_init, n_lat),
              _swa_attention(qkv, swa_sink[l], n_lat),
              _gqa_attention(qkv, gqt, gvt, n_lat))
        if not last:
            ys = _ctx_attention(qkv, ys, swa_sink[l], diff_lam[l], subw[l:l + 1], lam_init, n_lat)
        xs = _merge(xs, mod, nw1, ys, w_gate_b, b_gate3, w_branch_b, w_out_b, l, n_lat, n_rows)
        xs = _peer(xs, mod, nw2, wq_t, keys_b, u_b, v_t, final_w, l, n_lat, not last, last)
    return xs[:n_lat].reshape(batch, n_lat, d)
```

```python
import functools
import math

import numpy as np
import jax
import jax.numpy as jnp
from jax import lax
from jax.experimental import pallas as pl
from jax.experimental.pallas import tpu as pltpu

F32 = jnp.float32
BF16 = jnp.bfloat16

D_MODEL = 1024
GRID_W = 64
D_HEAD = 64
ROPE_THETA = 10000.0
EPS = 1e-6
NEG_INF = -1e30
LOG2E = math.log2(math.e)
NA_KH = 8
NA_KW = 16
DIFF_QK_DIM = 32
SWA_WINDOW = 128
N_BRANCH = 4
GROUP_W = 256
N_GROUPS = 10
PEER_HEADS = 8
N_KEYS = 128
PEER_KEY_DIM = 128
PEER_TOPK = 16
QBLK = 128
NA_KBLKS = 5
VMEM_LIMIT = 56 * 1024 * 1024


def _dot(a, b):
    return jnp.dot(a, b, preferred_element_type=F32)


def _dot_nt(a, b):
    return lax.dot_general(a, b, (((1,), (1,)), ((), ())), preferred_element_type=F32)


def _params(sem, vmem=VMEM_LIMIT):
    return pltpu.CompilerParams(dimension_semantics=sem, vmem_limit_bytes=vmem)


def _const_spec(shape, index_map):
    return pl.BlockSpec(shape, index_map, pipeline_mode=pl.Buffered(1))


def _mod_rows(mod_ref, k, is_ctx):
    lo, hi = k * D_MODEL, (k + 1) * D_MODEL
    return jnp.where(is_ctx, mod_ref[1:2, lo:hi], mod_ref[0:1, lo:hi])


def _norm_modulate(x, nw, shift, scale):
    ms = jnp.mean(x * x, axis=-1, keepdims=True)
    h = (x * lax.rsqrt(ms + EPS)) * nw
    return h * (1.0 + scale) + shift


def _group_mean_sq(y, width):
    w = y.shape[1]
    y2 = y * y
    r = lax.broadcasted_iota(jnp.int32, (w, w), 0) // width
    c = lax.broadcasted_iota(jnp.int32, (w, w), 1) // width
    ones = jnp.where(r == c, 1.0, 0.0).astype(BF16)
    hi = y2.astype(BF16)
    lo = (y2 - hi.astype(F32)).astype(BF16)
    return (_dot(hi, ones) + _dot(lo, ones)) * (1.0 / width)


def _rope(y, cos, sin_signed, quarter):
    w = y.shape[1]
    lane = lax.broadcasted_iota(jnp.int32, y.shape, 1)
    first = (lane % (2 * quarter)) < quarter
    rot = jnp.where(first, pltpu.roll(y, w - quarter, 1), pltpu.roll(y, quarter, 1))
    return y * cos + rot * sin_signed


def _softmax_pv_heads(scores, values, sinks=None):
    probs, denoms = [], []
    for n, s in enumerate(scores):
        m = jnp.max(s, axis=1, keepdims=True)
        if sinks is not None:
            m = jnp.maximum(m, sinks[n])
        p = jnp.exp(s - m)
        l = jnp.sum(p, axis=1, keepdims=True)
        if sinks is not None:
            l = l + jnp.exp(sinks[n] - m)
        probs.append(p.astype(BF16))
        denoms.append(l)
    return [_dot(p, v) / l for p, v, l in zip(probs, values, denoms)]


def _softmax_pv(s, v, sink=None):
    return _softmax_pv_heads([s], [v], None if sink is None else [sink])[0]


def _diff_lambda(lam_ref, lam_init):
    lf = lam_ref[...]
    a = jnp.sum(lf[0:1] * lf[1:2], axis=1, keepdims=True)
    b = jnp.sum(lf[2:3] * lf[3:4], axis=1, keepdims=True)
    return jnp.exp(a) - jnp.exp(b) + lam_init


def _diff_finish(o, subw, lam_init):
    ms = _group_mean_sq(o, D_HEAD)
    return (o * lax.rsqrt(ms + EPS)) * subw * (1.0 - lam_init)


def _mod_kernel(c_ref, w_ref, b_ref, o_ref):
    c = c_ref[...]
    s = c * jax.nn.sigmoid(c)
    o_ref[...] = jnp.dot(s, w_ref[...], precision=lax.Precision.HIGHEST,
                         preferred_element_type=F32) + b_ref[...]


def _modulation(c2, ada_w, ada_b, layer):
    n_out = ada_w.shape[2]
    tn = 1536
    return pl.pallas_call(
        _mod_kernel,
        grid=(n_out // tn,),
        in_specs=[pl.BlockSpec((8, D_MODEL), lambda j: (0, 0)),
                  pl.BlockSpec((None, D_MODEL, tn), lambda j: (layer, 0, j)),
                  pl.BlockSpec((None, 1, tn), lambda j: (layer, 0, j))],
        out_specs=pl.BlockSpec((8, tn), lambda j: (0, j)),
        out_shape=jax.ShapeDtypeStruct((8, n_out), F32),
        compiler_params=_params(("arbitrary",)),
        name="adaln_mod",
    )(c2, ada_w, ada_b.reshape(ada_b.shape[0], 1, n_out))


def _proj_kernel(x_ref, mod_ref, nw_ref, w_ref, c64_ref, s64_ref, c32_ref, s32_ref, qkw_ref,
                 o_ref, dqt_ref, dvt_ref, gqt_ref, gvt_ref, *, n_lat, tm):
    i = pl.program_id(0)
    row = i * tm + lax.broadcasted_iota(jnp.int32, (tm, 1), 0)
    is_ctx = row >= n_lat
    h = _norm_modulate(x_ref[...], nw_ref[...], _mod_rows(mod_ref, 0, is_ctx), _mod_rows(mod_ref, 1, is_ctx))
    hb = h.astype(BF16)
    c64, s64 = c64_ref[...], s64_ref[...]
    c32, s32 = c32_ref[...], s32_ref[...]
    half = GROUP_W // 2
    sc64 = D_HEAD ** -0.5
    sc32 = DIFF_QK_DIM ** -0.5

    def qk_norm(y, w):
        return (y * lax.rsqrt(_group_mean_sq(y, D_HEAD) + EPS)) * w

    for g in range(N_GROUPS):
        y = _dot(hb, w_ref[:, g * GROUP_W:(g + 1) * GROUP_W])
        if g == 0:
            y = y * sc64
        elif g == 3:
            y = _rope(y, c32, s32, DIFF_QK_DIM // 4) * sc32
            dqt_ref[...] = (y * LOG2E).T.astype(BF16)
        elif g == 5:
            dvt_ref[...] = y.T.astype(BF16)
        elif g == 4:
            y = _rope(y, c32, s32, DIFF_QK_DIM // 4)
        elif g == 6:
            y = _rope(y, c64, s64, D_HEAD // 4) * sc64
        elif g == 7:
            k = _rope(y[:, :half], c64[:, :half], s64[:, :half], D_HEAD // 4)
            y = jnp.concatenate([k, y[:, half:]], axis=1)
        elif g == 8:
            y = _rope(qk_norm(y, qkw_ref[0:1, :]), c64, s64, D_HEAD // 4) * sc64
            gqt_ref[...] = (y * LOG2E).T.astype(BF16)
        elif g == 9:
            k = _rope(qk_norm(y[:, :half], qkw_ref[1:2, :half]), c64[:, :half], s64[:, :half], D_HEAD // 4)
            gvt_ref[...] = y[:, half:].T.astype(BF16)
            y = jnp.concatenate([k, y[:, half:]], axis=1)
        o_ref[:, g * GROUP_W:(g + 1) * GROUP_W] = y.astype(BF16)


def _projection(xs, mod, nw, w_in, tabs, qkw, layer, n_lat):
    n = xs.shape[0]
    tm = _row_tile(n)
    width = N_GROUPS * GROUP_W
    row_spec = lambda w: pl.BlockSpec((tm, w), lambda i: (i, 0))
    col_spec = lambda w: pl.BlockSpec((w, tm), lambda i: (0, i))
    t_shape = lambda w: jax.ShapeDtypeStruct((w, n), BF16)
    return pl.pallas_call(
        functools.partial(_proj_kernel, n_lat=n_lat, tm=tm),
        grid=(n // tm,),
        in_specs=[row_spec(D_MODEL),
                  _const_spec((8, 6 * D_MODEL), lambda i: (0, 0)),
                  _const_spec((None, 1, D_MODEL), lambda i: (layer, 0, 0)),
                  _const_spec((None, D_MODEL, width), lambda i: (layer, 0, 0)),
                  row_spec(GROUP_W), row_spec(GROUP_W), row_spec(GROUP_W), row_spec(GROUP_W),
                  _const_spec((None, 2, GROUP_W), lambda i: (layer, 0, 0))],
        out_specs=[row_spec(width), col_spec(GROUP_W), col_spec(GROUP_W), col_spec(GROUP_W),
                   col_spec(GROUP_W // 2)],
        out_shape=[jax.ShapeDtypeStruct((n, width), BF16), t_shape(GROUP_W), t_shape(GROUP_W), t_shape(GROUP_W),
                   t_shape(GROUP_W // 2)],
        compiler_params=_params(("parallel",)),
        name="proj",
    )(xs, mod, nw, w_in, *tabs, qkw)


def _na_kernel(q_ref, *refs):
    k_refs = refs[:NA_KBLKS]
    v_refs = refs[NA_KBLKS:2 * NA_KBLKS]
    kc_ref, vc_ref, bias_ref, o_ref = refs[2 * NA_KBLKS:]
    q = q_ref[...]
    ks = [r[...] for r in k_refs] + [kc_ref[...]]
    vs = [r[...] for r in v_refs] + [vc_ref[...]]
    scores, values = [], []
    for h in range(4):
        sl = slice(h * D_HEAD, (h + 1) * D_HEAD)
        qh = q[:, sl]
        s_loc = jnp.concatenate([_dot_nt(qh, k[:, sl]) for k in ks[:NA_KBLKS]], axis=1) + bias_ref[0, h]
        scores.append(jnp.concatenate([s_loc, _dot_nt(qh, ks[NA_KBLKS][:, sl])], axis=1))
        values.append(jnp.concatenate([v[:, sl] for v in vs], axis=0))
    o_ref[...] = jnp.concatenate(_softmax_pv_heads(scores, values), axis=1).astype(BF16)


def _na_bias(rpb, n_lat):
    nb = n_lat // QBLK
    rows = n_lat // GRID_W
    rep = np.array([0, 1, 2, nb - 2, nb - 1])
    wstart = np.clip(rep - 2, 0, nb - NA_KBLKS)
    qi = np.arange(QBLK)
    kj = np.arange(NA_KBLKS * QBLK)
    qrow = 2 * rep[:, None] + qi[None, :] // GRID_W
    qcol = qi % GRID_W
    krow = 2 * wstart[:, None] + kj[None, :] // GRID_W
    kcol = kj % GRID_W
    rs = np.clip(qrow - NA_KH // 2, 0, rows - NA_KH)
    row_ok = (krow[:, None, :] >= rs[:, :, None]) & (krow[:, None, :] < rs[:, :, None] + NA_KH)
    cs = np.clip(qcol - NA_KW // 2, 0, GRID_W - NA_KW)
    col_ok = (kcol[None, :] >= cs[:, None]) & (kcol[None, :] < cs[:, None] + NA_KW)
    ok = row_ok & col_ok[None]
    gc = np.arange(GRID_W)
    cidx = np.clip(gc[None, :] - gc[:, None], -(NA_KW - 1), NA_KW - 1) + (NA_KW - 1)
    onehot = (cidx.reshape(-1)[None, :] == np.arange(2 * NA_KW - 1)[:, None]).astype(np.float32)
    table = jnp.einsum("hdc,cn->hdn", rpb.astype(F32), onehot, precision=lax.Precision.HIGHEST)
    table = table.reshape(rpb.shape[0], 2 * NA_KH - 1, GRID_W, GRID_W)
    zero = jnp.zeros((rpb.shape[0], GRID_W, GRID_W), F32)
    classes = []
    for c in range(len(rep)):
        q_rows = []
        for qr in range(QBLK // GRID_W):
            pieces = []
            for kr in range(NA_KBLKS * QBLK // GRID_W):
                d = int(2 * wstart[c] + kr - (2 * rep[c] + qr) + NA_KH - 1)
                pieces.append(table[:, d] if 0 <= d <= 2 * NA_KH - 2 else zero)
            q_rows.append(jnp.concatenate(pieces, axis=-1))
        classes.append(jnp.concatenate(q_rows, axis=-2))
    return jnp.where(ok[:, None], jnp.stack(classes), NEG_INF)


def _na_attention(qkv, bias, n_lat):
    n = qkv.shape[0]
    nb = n_lat // QBLK
    cblk = n_lat // 256

    def cls(b):
        return jnp.where(b < 2, b, jnp.where(b >= nb - 2, b - (nb - 5), 2))

    def kblk(j, col):
        return pl.BlockSpec((QBLK, GROUP_W), lambda b: (jnp.clip(b - 2, 0, nb - NA_KBLKS) + j, col))

    in_specs = ([pl.BlockSpec((QBLK, GROUP_W), lambda b: (b, 0))]
                + [kblk(j, 1) for j in range(NA_KBLKS)] + [kblk(j, 2) for j in range(NA_KBLKS)]
                + [pl.BlockSpec((256, GROUP_W), lambda b: (cblk, 1)),
                   pl.BlockSpec((256, GROUP_W), lambda b: (cblk, 2)),
                   pl.BlockSpec((1, 4, QBLK, NA_KBLKS * QBLK), lambda b: (cls(b), 0, 0, 0))])
    return pl.pallas_call(
        _na_kernel,
        grid=(nb,),
        in_specs=in_specs,
        out_specs=pl.BlockSpec((QBLK, GROUP_W), lambda b: (b, 0)),
        out_shape=jax.ShapeDtypeStruct((n, GROUP_W), BF16),
        compiler_params=_params(("parallel",)),
        name="na_attn",
    )(*([qkv] * (3 + 2 * NA_KBLKS)), bias)


def _swa_kernel(sink_ref, q_ref, km_ref, k0_ref, kp_ref, kvc_ref, o_ref, *, nb):
    b = pl.program_id(0)
    q = q_ref[...]
    blocks = [km_ref[...], k0_ref[...], kp_ref[...], kvc_ref[...]]
    row = lax.broadcasted_iota(jnp.int32, (QBLK, QBLK), 0)
    col = lax.broadcasted_iota(jnp.int32, (QBLK, QBLK), 1)
    ok_prev = jnp.logical_and(col >= row, b > 0)
    ok_next = jnp.logical_and(col <= row, b < nb - 1)
    scores, values = [], []
    for h in range(4):
        g = h // 2
        qh = q[:, h * D_HEAD:(h + 1) * D_HEAD]
        ksl = slice(g * D_HEAD, (g + 1) * D_HEAD)
        vsl = slice(GROUP_W // 2 + g * D_HEAD, GROUP_W // 2 + (g + 1) * D_HEAD)
        scores.append(jnp.concatenate([jnp.where(ok_prev, _dot_nt(qh, blocks[0][:, ksl]), NEG_INF),
                                       _dot_nt(qh, blocks[1][:, ksl]),
                                       jnp.where(ok_next, _dot_nt(qh, blocks[2][:, ksl]), NEG_INF),
                                       _dot_nt(qh, blocks[3][:, ksl])], axis=1))
        values.append(jnp.concatenate([blk[:, vsl] for blk in blocks], axis=0))
    outs = _softmax_pv_heads(scores, values, [sink_ref[h] for h in range(4)])
    o_ref[...] = jnp.concatenate(outs, axis=1).astype(BF16)


def _swa_attention(qkv, sink, n_lat):
    n = qkv.shape[0]
    nb = n_lat // QBLK
    cblk = n_lat // 256
    return pl.pallas_call(
        functools.partial(_swa_kernel, nb=nb),
        grid=(nb,),
        in_specs=[pl.BlockSpec(memory_space=pltpu.SMEM),
                  pl.BlockSpec((QBLK, GROUP_W), lambda b: (b, 6)),
                  pl.BlockSpec((QBLK, GROUP_W), lambda b: (jnp.maximum(b - 1, 0), 7)),
                  pl.BlockSpec((QBLK, GROUP_W), lambda b: (b, 7)),
                  pl.BlockSpec((QBLK, GROUP_W), lambda b: (jnp.minimum(b + 1, nb - 1), 7)),
                  pl.BlockSpec((256, GROUP_W), lambda b: (cblk, 7))],
        out_specs=pl.BlockSpec((QBLK, GROUP_W), lambda b: (b, 0)),
        out_shape=jax.ShapeDtypeStruct((n, GROUP_W), BF16),
        compiler_params=_params(("parallel",)),
        name="swa_attn",
    )(sink, qkv, qkv, qkv, qkv, qkv)


ONES_ROWS = 16
ACC_ROWS = D_HEAD + ONES_ROWS
FLASH_QUERY_CHUNK = 512


def _flash_units(units, qtu_ref, m_ref, acc_ref):
    tq = qtu_ref.shape[2]
    width = min(tq, FLASH_QUERY_CHUNK)
    units = [(u, k_lanes, vt_ext, slice(c * width, (c + 1) * width))
             for (u, k_lanes, vt_ext) in units for c in range(tq // width)]
    n = len(units)
    st, pt, alpha = {}, {}, {}
    for t in range(n + 2):
        if t < n:
            u, k_lanes, _, cols = units[t]
            st[t] = _dot(k_lanes, qtu_ref[u, :, cols])
        if 0 <= t - 1 < n:
            u, _, _, cols = units[t - 1]
            s = st.pop(t - 1)
            m_old = m_ref[u, :, cols]
            m_new = jnp.maximum(m_old, jnp.max(s, axis=0, keepdims=True))
            alpha[t - 1] = jnp.exp2(m_old - m_new)
            pt[t - 1] = jnp.exp2(s - m_new).astype(BF16)
            m_ref[u, :, cols] = m_new
        if 0 <= t - 2 < n:
            u, _, vt_ext, cols = units[t - 2]
            acc_ref[u, :, cols] = alpha.pop(t - 2) * acc_ref[u, :, cols] + _dot(vt_ext, pt.pop(t - 2))


def _flash_init(m_ref, acc_ref):
    m_ref[...] = jnp.full(m_ref.shape, NEG_INF, F32)
    acc_ref[...] = jnp.zeros(acc_ref.shape, F32)


def _with_ones(vt_h):
    return jnp.concatenate([vt_h, jnp.ones((ONES_ROWS, vt_h.shape[1]), BF16)], axis=0)


def _normalised(acc):
    return acc[:D_HEAD] / acc[D_HEAD:D_HEAD + 1]


def _diff_flash_kernel(lam_ref, subw_ref, qt_ref, k_ref, vt_ref, o_ref, qtu_ref, m_ref, acc_ref, *, lam_init):
    j = pl.program_id(1)
    lanes = 128

    @pl.when(j == 0)
    def _():
        _flash_init(m_ref, acc_ref)
        row = lax.broadcasted_iota(jnp.int32, (lanes, qt_ref.shape[1]), 0)
        for u in range(8):
            grp = qt_ref[(u // 4) * lanes:(u // 4 + 1) * lanes, :].astype(F32)
            lo = (u % 4) * DIFF_QK_DIM
            own = jnp.logical_and(row >= lo, row < lo + DIFF_QK_DIM)
            qtu_ref[u] = jnp.where(own, grp, 0.0).astype(BF16)

    k, vt = k_ref[...], vt_ref[...]
    vt_ext = [_with_ones(vt[h * D_HEAD:(h + 1) * D_HEAD]) for h in range(4)]
    _flash_units([(u, k[:, (u // 4) * lanes:(u // 4 + 1) * lanes], vt_ext[u // 2]) for u in range(8)],
                 qtu_ref, m_ref, acc_ref)

    @pl.when(j == pl.num_programs(1) - 1)
    def _():
        lam = _diff_lambda(lam_ref, lam_init)
        heads = []
        for h in range(4):
            o = _normalised(acc_ref[2 * h]) - lam * _normalised(acc_ref[2 * h + 1])
            heads.append(o * lax.rsqrt(jnp.mean(o * o, axis=0, keepdims=True) + EPS))
        ot = jnp.concatenate(heads, axis=0)
        o_ref[...] = (ot.T * subw_ref[...] * (1.0 - lam_init)).astype(BF16)


def _gqa_flash_kernel(qt_ref, k_ref, vt_ref, o_ref, qtu_ref, m_ref, acc_ref):
    j = pl.program_id(1)

    @pl.when(j == 0)
    def _():
        _flash_init(m_ref, acc_ref)
        zeros = jnp.zeros((D_HEAD, qt_ref.shape[1]), BF16)
        for h in range(4):
            qh = qt_ref[h * D_HEAD:(h + 1) * D_HEAD, :]
            qtu_ref[h] = jnp.concatenate([qh, zeros] if h < 2 else [zeros, qh], axis=0)

    k, vt = k_ref[...], vt_ref[...]
    vt_ext = [_with_ones(vt[g * D_HEAD:(g + 1) * D_HEAD]) for g in range(2)]
    _flash_units([(h, k, vt_ext[h // 2]) for h in range(4)], qtu_ref, m_ref, acc_ref)

    @pl.when(j == pl.num_programs(1) - 1)
    def _():
        ot = jnp.concatenate([_normalised(acc_ref[h]) for h in range(4)], axis=0)
        o_ref[...] = ot.T.astype(BF16)


def _flash_tiles(n, n_lat):
    tq = 512 if n_lat % 512 == 0 else 256
    tk = next(t for t in (1280, 256) if n % t == 0)
    return tq, tk


def _flash_scratch(units, tq):
    return [pltpu.VMEM((units, 128, tq), BF16), pltpu.VMEM((units, 1, tq), F32),
            pltpu.VMEM((units, ACC_ROWS, tq), F32)]


def _diff_attention(qkv, qt, vt, lam_p, subw, lam_init, n_lat):
    n = qkv.shape[0]
    tq, tk = _flash_tiles(n, n_lat)
    return pl.pallas_call(
        functools.partial(_diff_flash_kernel, lam_init=lam_init),
        grid=(n_lat // tq, n // tk),
        in_specs=[pl.BlockSpec((4, DIFF_QK_DIM), lambda i, j: (0, 0)),
                  pl.BlockSpec((1, GROUP_W), lambda i, j: (0, 0)),
                  pl.BlockSpec((GROUP_W, tq), lambda i, j: (0, i)),
                  pl.BlockSpec((tk, GROUP_W), lambda i, j: (j, 4)),
                  pl.BlockSpec((GROUP_W, tk), lambda i, j: (0, j))],
        out_specs=pl.BlockSpec((tq, GROUP_W), lambda i, j: (i, 0)),
        out_shape=jax.ShapeDtypeStruct((n, GROUP_W), BF16),
        scratch_shapes=_flash_scratch(8, tq),
        compiler_params=_params(("parallel", "arbitrary")),
        name="diff_attn",
    )(lam_p, subw, qt, qkv, vt)


def _gqa_attention(qkv, qt, vt, n_lat):
    n = qkv.shape[0]
    tq, tk = _flash_tiles(n, n_lat)
    half = GROUP_W // 2
    return pl.pallas_call(
        _gqa_flash_kernel,
        grid=(n_lat // tq, n // tk),
        in_specs=[pl.BlockSpec((GROUP_W, tq), lambda i, j: (0, i)),
                  pl.BlockSpec((tk, half), lambda i, j: (j, 9 * GROUP_W // half)),
                  pl.BlockSpec((half, tk), lambda i, j: (0, j))],
        out_specs=pl.BlockSpec((tq, GROUP_W), lambda i, j: (i, 0)),
        out_shape=jax.ShapeDtypeStruct((n, GROUP_W), BF16),
        scratch_shapes=_flash_scratch(4, tq),
        compiler_params=_params(("parallel", "arbitrary")),
        name="gqa_attn",
    )(qt, qkv, vt)


def _ctx_kernel(sink_ref, lam_ref, subw_ref, qkv_ref, a0, a1, a2, a3, o_na, o_df, o_sw, o_gq, *, lam_init):
    del a0, a1, a2, a3
    grp = lambda g: qkv_ref[:, g * GROUP_W:(g + 1) * GROUP_W]
    hs = lambda h: slice(h * D_HEAD, (h + 1) * D_HEAD)
    half = GROUP_W // 2

    q, k, v = grp(0), grp(1), grp(2)
    o_na[...] = jnp.concatenate([_softmax_pv(_dot_nt(q[:, hs(h)], k[:, hs(h)]), v[:, hs(h)])
                                 for h in range(4)], axis=1).astype(BF16)

    q, k, v = grp(3), grp(4), grp(5)
    lam = _diff_lambda(lam_ref, lam_init)
    heads = []
    for h in range(4):
        maps = []
        for m in range(2):
            qs = slice((2 * h + m) * DIFF_QK_DIM, (2 * h + m + 1) * DIFF_QK_DIM)
            maps.append(_softmax_pv(_dot_nt(q[:, qs], k[:, qs]), v[:, hs(h)]))
        heads.append(maps[0] - lam * maps[1])
    o_df[...] = _diff_finish(jnp.concatenate(heads, axis=1), subw_ref[...], lam_init).astype(BF16)

    for q_g, kv_g, out, use_sink in ((6, 7, o_sw, True), (8, 9, o_gq, False)):
        q, kv = grp(q_g), grp(kv_g)
        outs = []
        for h in range(4):
            g = h // 2
            s = _dot_nt(q[:, hs(h)], kv[:, hs(g)])
            vh = kv[:, half + g * D_HEAD:half + (g + 1) * D_HEAD]
            outs.append(_softmax_pv(s, vh, sink=sink_ref[h] if use_sink else None))
        out[...] = jnp.concatenate(outs, axis=1).astype(BF16)


def _ctx_attention(qkv, ys, sink, lam_p, subw, lam_init, n_lat):
    n = qkv.shape[0]
    n_ctx = n - n_lat
    cblk = n_lat // n_ctx
    y_spec = pl.BlockSpec((n_ctx, GROUP_W), lambda i: (cblk, 0))
    any_spec = pl.BlockSpec(memory_space=pl.ANY)
    return pl.pallas_call(
        functools.partial(_ctx_kernel, lam_init=lam_init),
        grid=(1,),
        in_specs=[pl.BlockSpec(memory_space=pltpu.SMEM),
                  pl.BlockSpec((4, DIFF_QK_DIM), lambda i: (0, 0)),
                  pl.BlockSpec((1, GROUP_W), lambda i: (0, 0)),
                  pl.BlockSpec((n_ctx, N_GROUPS * GROUP_W), lambda i: (cblk, 0)),
                  any_spec, any_spec, any_spec, any_spec],
        out_specs=[y_spec] * 4,
        out_shape=[jax.ShapeDtypeStruct((n, GROUP_W), BF16)] * 4,
        input_output_aliases={4: 0, 5: 1, 6: 2, 7: 3},
        compiler_params=_params(("arbitrary",)),
        name="ctx_attn",
    )(sink, lam_p, subw, qkv, *ys)


def _merge_kernel(x_ref, mod_ref, nw_ref, y0, y1, y2, y3, wg_ref, bg_ref, wb_ref, wo_ref, o_ref, *, n_lat, tm):
    i = pl.program_id(0)
    row = i * tm + lax.broadcasted_iota(jnp.int32, (tm, 1), 0)
    is_ctx = row >= n_lat
    x = x_ref[...]
    h = _norm_modulate(x, nw_ref[...], _mod_rows(mod_ref, 0, is_ctx), _mod_rows(mod_ref, 1, is_ctx))
    hb = h.astype(BF16)
    acc = jnp.zeros((tm, D_MODEL), F32)
    for n, y_ref in enumerate((y0, y1, y2, y3)):
        cols = slice(n * D_MODEL, (n + 1) * D_MODEL)
        gate = jax.nn.sigmoid(_dot(hb, wg_ref[:, cols]) + bg_ref[:, cols])
        acc = acc + gate * _dot(y_ref[...], wb_ref[n])
    out = _dot(acc.astype(BF16), wo_ref[...])
    o_ref[...] = x + _mod_rows(mod_ref, 2, is_ctx) * out


def _merge(xs, mod, nw, ys, w_gate, b_gate, w_branch, w_out, layer, n_lat, n_rows):
    tm = _row_tile(n_rows)
    row_spec = lambda w: pl.BlockSpec((tm, w), lambda i: (i, 0))
    return pl.pallas_call(
        functools.partial(_merge_kernel, n_lat=n_lat, tm=tm),
        grid=(n_rows // tm,),
        in_specs=[row_spec(D_MODEL),
                  _const_spec((8, 6 * D_MODEL), lambda i: (0, 0)),
                  _const_spec((None, 1, D_MODEL), lambda i: (layer, 0, 0)),
                  row_spec(GROUP_W), row_spec(GROUP_W), row_spec(GROUP_W), row_spec(GROUP_W),
                  _const_spec((None, D_MODEL, N_BRANCH * D_MODEL), lambda i: (layer, 0, 0)),
                  _const_spec((None, 1, N_BRANCH * D_MODEL), lambda i: (layer, 0, 0)),
                  _const_spec((None, N_BRANCH, GROUP_W, D_MODEL), lambda i: (layer, 0, 0, 0)),
                  _const_spec((None, D_MODEL, D_MODEL), lambda i: (layer, 0, 0))],
        out_specs=row_spec(D_MODEL),
        out_shape=jax.ShapeDtypeStruct((n_rows, D_MODEL), F32),
        compiler_params=_params(("parallel",)),
        name="merge",
    )(xs, mod, nw, *ys, w_gate, b_gate, w_branch, w_out)


PEER_UNRANKED = 64.0


def _top_desc(ref, count, rank_ref=None):
    vals = []
    if rank_ref is not None:
        rank_ref[...] = jnp.full(rank_ref.shape, PEER_UNRANKED, F32)
    for r in range(count):
        cur = ref[...]
        m = jnp.max(cur, axis=0, keepdims=True)
        vals.append(m)
        hit = cur == m
        if rank_ref is not None:
            rank_ref[...] = jnp.where(hit, float(r), rank_ref[...])
        if r + 1 < count:
            ref[...] = jnp.where(hit, -jnp.inf, cur)
    return vals


def _gelu_tanh(x):
    y2 = (2.0 * math.sqrt(2.0 / math.pi)) * (x + 0.044715 * (x * x * x))
    return x / (1.0 + jnp.exp(-y2))


PEER_EXTRACT = PEER_TOPK + 1
_PAIR_RANKS = [(a, b) for a in range(PEER_EXTRACT) for b in range(PEER_EXTRACT) if (a + 1) * (b + 1) <= PEER_EXTRACT]
_N_CAND = -(-len(_PAIR_RANKS) // 8) * 8
PEER_TOKEN_CHUNK = 256
PEER_EXPERT_BLOCK = 256


def _peer_kernel(x_ref, mod_ref, nw_ref, wqt_ref, keys_ref, u_ref, vt_ref, fw_ref, base_ref, o_ref,
                 h2t_s, cnt_s, e1_s, rank2_s, e2_s, work_s, rank_s, cand_s, p_s, acc_s,
                 *, grp, tm, te, final_norm):
    del base_ref
    e = pl.program_id(1)
    mod_row = lambda k: mod_ref[grp:grp + 1, k * D_MODEL:(k + 1) * D_MODEL]

    @pl.when(e == 0)
    def _prepare():
        h2 = _norm_modulate(x_ref[...], nw_ref[...], mod_row(3), mod_row(4))
        h2t_s[...] = h2.T.astype(BF16)
        hbt = h2t_s[...]
        cand_s[...] = jnp.full(cand_s.shape, -jnp.inf, F32)
        for hh in range(PEER_HEADS):
            tops, scores = [], []
            for p in range(2):
                idx = 2 * hh + p
                qt = _dot(wqt_ref[idx * PEER_KEY_DIM:(idx + 1) * PEER_KEY_DIM, :], hbt)
                s = _dot(keys_ref[p, hh], qt.astype(BF16))
                scores.append(s)
                work_s[...] = s
                tops.append(_top_desc(work_s, PEER_EXTRACT, rank_s if p == 1 else None))
            for c, (ra, rb) in enumerate(_PAIR_RANKS):
                cand_s[c:c + 1, :] = tops[0][ra] + tops[1][rb]
            best = _top_desc(cand_s, PEER_EXTRACT)
            z = jnp.zeros_like(best[0])
            for bk in best[:PEER_TOPK]:
                z = z + jnp.exp(bk - best[0])
            tau = 0.5 * (best[PEER_TOPK - 1] + best[PEER_TOPK])
            need = tau - scores[0]
            cnt = jnp.zeros_like(need)
            for r, a2 in enumerate(tops[1]):
                cnt = jnp.where(a2 >= need, float(r + 1), cnt)
            cnt_s[hh] = cnt
            rank2_s[hh] = rank_s[...].astype(BF16)
            e1_s[hh] = jnp.exp(scores[0] - tops[0][0]) / z
            e2_s[hh] = jnp.exp(scores[1] - tops[1][0]).astype(BF16)
        acc_s[...] = jnp.zeros(acc_s.shape, F32)

    n_blk = te // PEER_EXPERT_BLOCK
    per_blk = PEER_EXPERT_BLOCK // N_KEYS
    span = lambda b: slice(b * PEER_EXPERT_BLOCK, (b + 1) * PEER_EXPERT_BLOCK)
    acts = {0: _dot(u_ref[span(0), :], h2t_s[...])}
    for b in range(n_blk):
        if b + 1 < n_blk:
            acts[b + 1] = _dot(u_ref[span(b + 1), :], h2t_s[...])
        a = acts.pop(b)
        for ii in range(per_blk):
            i = e * (te // N_KEYS) + b * per_blk + ii
            rows = slice(ii * N_KEYS, (ii + 1) * N_KEYS)
            prow = slice(b * PEER_EXPERT_BLOCK + ii * N_KEYS, b * PEER_EXPERT_BLOCK + (ii + 1) * N_KEYS)
            for tc in range(tm // PEER_TOKEN_CHUNK):
                cols = slice(tc * PEER_TOKEN_CHUNK, (tc + 1) * PEER_TOKEN_CHUNK)
                w = jnp.zeros((N_KEYS, PEER_TOKEN_CHUNK), BF16)
                for hh in range(PEER_HEADS):
                    sel = rank2_s[hh, :, cols] < cnt_s[hh, pl.ds(i, 1), cols].astype(BF16)
                    gate = jnp.where(sel, e2_s[hh, :, cols], jnp.zeros((), BF16))
                    w = w + gate * e1_s[hh, pl.ds(i, 1), cols].astype(BF16)
                p_s[prow, cols] = w * _gelu_tanh(a[rows, cols]).astype(BF16)
    acc_s[...] += _dot(vt_ref[...], p_s[...])

    @pl.when(e == pl.num_programs(1) - 1)
    def _finish():
        xn = x_ref[...] + mod_row(5) * acc_s[...].T
        if final_norm:
            ms = jnp.mean(xn * xn, axis=-1, keepdims=True)
            xn = (xn * lax.rsqrt(ms + EPS)) * fw_ref[...]
        o_ref[...] = xn


def _peer_call(xs, base, mod, nw, wq_t, keys, u, v_t, final_w, layer, grp, row0, n_rows, n_out, final_norm):
    tm = 512 if n_rows % 512 == 0 else 256
    te = 2048
    n_exp = u.shape[1]
    blk0 = row0 // tm
    hs = (PEER_HEADS, N_KEYS, tm)
    row_spec = pl.BlockSpec((tm, D_MODEL), lambda t, e: (blk0 + t, 0))
    return pl.pallas_call(
        functools.partial(_peer_kernel, grp=grp, tm=tm, te=te, final_norm=final_norm),
        grid=(n_rows // tm, n_exp // te),
        in_specs=[row_spec,
                  _const_spec((8, 6 * D_MODEL), lambda t, e: (0, 0)),
                  _const_spec((None, 1, D_MODEL), lambda t, e: (layer, 0, 0)),
                  _const_spec((None, 2 * PEER_HEADS * PEER_KEY_DIM, D_MODEL), lambda t, e: (layer, 0, 0)),
                  _const_spec((None, 2, PEER_HEADS, N_KEYS, PEER_KEY_DIM), lambda t, e: (layer, 0, 0, 0, 0)),
                  pl.BlockSpec((None, te, D_MODEL), lambda t, e: (layer, e, 0)),
                  pl.BlockSpec((None, D_MODEL, te), lambda t, e: (layer, 0, e)),
                  _const_spec((1, D_MODEL), lambda t, e: (0, 0)),
                  pl.BlockSpec(memory_space=pl.ANY)],
        out_specs=row_spec,
        out_shape=jax.ShapeDtypeStruct((n_out, D_MODEL), F32),
        input_output_aliases={} if base is xs else {8: 0},
        scratch_shapes=[pltpu.VMEM((D_MODEL, tm), BF16),
                        pltpu.VMEM(hs, F32), pltpu.VMEM(hs, F32), pltpu.VMEM(hs, BF16), pltpu.VMEM(hs, BF16),
                        pltpu.VMEM((N_KEYS, tm), F32), pltpu.VMEM((N_KEYS, tm), F32),
                        pltpu.VMEM((_N_CAND, tm), F32),
                        pltpu.VMEM((te, tm), BF16),
                        pltpu.VMEM((D_MODEL, tm), F32)],
        compiler_params=_params(("parallel", "arbitrary")),
        name="peer",
    )(xs, mod, nw, wq_t, keys, u, v_t, final_w, base)


def _peer(xs, mod, nw, wq_t, keys, u, v_t, final_w, layer, n_lat, with_ctx, final_norm):
    n_out = xs.shape[0] if with_ctx else n_lat
    out = _peer_call(xs, xs, mod, nw, wq_t, keys, u, v_t, final_w, layer, 0, 0, n_lat, n_out, final_norm)
    if with_ctx:
        out = _peer_call(xs, out, mod, nw, wq_t, keys, u, v_t, final_w, layer, 1, n_lat, xs.shape[0] - n_lat,
                         n_out, final_norm)
    return out


def _row_tile(n):
    for tm in (640, 512, 256):
        if n % tm == 0:
            return tm
    raise ValueError(f"unsupported row count {n}")


def _rope_tables(n_lat, n_ctx, dim):
    t = jnp.arange(n_lat, dtype=jnp.int32)
    row = (t // GRID_W).astype(F32)
    col = (t % GRID_W).astype(F32)
    n_freq = dim // 4
    inv_freq = ROPE_THETA ** (-jnp.arange(n_freq, dtype=F32) / n_freq)
    ang_r = row[:, None] * inv_freq[None, :]
    ang_c = col[:, None] * inv_freq[None, :]
    ang = jnp.concatenate([ang_r, ang_r, ang_c, ang_c], axis=-1)
    sign = np.where((np.arange(dim) % (dim // 2)) < dim // 4, -1.0, 1.0).astype(np.float32)
    cos = jnp.concatenate([jnp.cos(ang), jnp.ones((n_ctx, dim), F32)], axis=0)
    sin = jnp.concatenate([jnp.sin(ang) * sign, jnp.zeros((n_ctx, dim), F32)], axis=0)
    reps = GROUP_W // dim
    return jnp.tile(cos, (1, reps)), jnp.tile(sin, (1, reps))


def kernel(x, c, ctx, c_ctx, norm1_w, norm2_w, ada_w, ada_b, w_in, na_rpb, diff_lam, diff_subln_w, swa_sink,
           gqa_qk_norm_w, w_branch, w_gate, b_gate, w_out, peer_wq, peer_keys, peer_u, peer_v, final_norm_w):
    batch, n_lat, d = x.shape
    n_ctx = ctx.shape[1]
    depth = ada_w.shape[0]
    assert batch == 1 and d == D_MODEL and n_ctx == 256 and n_lat % 512 == 0 and n_lat >= NA_KBLKS * QBLK
    n = n_lat + n_ctx

    xs = jnp.concatenate([x[0], ctx[0]], axis=0)
    c2 = jnp.zeros((8, d), F32).at[0].set(c[0]).at[1].set(c_ctx)
    tabs = _rope_tables(n_lat, n_ctx, D_HEAD) + _rope_tables(n_lat, n_ctx, DIFF_QK_DIM)
    nw1 = norm1_w.reshape(depth, 1, d)
    nw2 = norm2_w.reshape(depth, 1, d)
    qkw = jnp.tile(gqa_qk_norm_w, (1, 1, GROUP_W // D_HEAD))
    subw = jnp.tile(diff_subln_w, (1, GROUP_W // D_HEAD))
    w_in_b = w_in.astype(BF16)
    w_gate_b = w_gate.astype(BF16)
    w_branch_b = w_branch.astype(BF16)
    w_out_b = w_out.astype(BF16)
    b_gate3 = b_gate.reshape(depth, 1, -1)
    wq_t = jnp.swapaxes(peer_wq, 1, 2).astype(BF16)
    keys_b = peer_keys.astype(BF16)
    u_b = peer_u.astype(BF16)
    v_t = jnp.swapaxes(peer_v, 1, 2).astype(BF16)
    final_w = final_norm_w.reshape(1, d)

    for l in range(depth):
        last = l == depth - 1
        lam_init = 0.8 - 0.6 * math.exp(-0.3 * l)
        n_rows = n_lat if last else n
        mod = _modulation(c2, ada_w, ada_b, l)
        qkv, dqt, dvt, gqt, gvt = _projection(xs, mod, nw1, w_in_b, tabs, qkw, l, n_lat)
        ys = (_na_attention(qkv, _na_bias(na_rpb[l], n_lat), n_lat),
              _diff_attention(qkv, dqt, dvt, diff_lam[l], subw[l:l + 1], lam_init, n_lat),
              _swa_attention(qkv, swa_sink[l], n_lat),
              _gqa_attention(qkv, gqt, gvt, n_lat))
        if not last:
            ys = _ctx_attention(qkv, ys, swa_sink[l], diff_lam[l], subw[l:l + 1], lam_init, n_lat)
        xs = _merge(xs, mod, nw1, ys, w_gate_b, b_gate3, w_branch_b, w_out_b, l, n_lat, n_rows)
        xs = _peer(xs, mod, nw2, wq_t, keys_b, u_b, v_t, final_w, l, n_lat, not last, last)
    return xs[:n_lat].reshape(batch, n_lat, d)
```

```python
import functools
import math

import numpy as np
import jax
import jax.numpy as jnp
from jax import lax
from jax.experimental import pallas as pl
from jax.experimental.pallas import tpu as pltpu

F32 = jnp.float32
BF16 = jnp.bfloat16

D_MODEL = 1024
GRID_W = 64
D_HEAD = 64
ROPE_THETA = 10000.0
EPS = 1e-6
NEG_INF = -1e30
LOG2E = math.log2(math.e)
NA_KH = 8
NA_KW = 16
DIFF_QK_DIM = 32
SWA_WINDOW = 128
N_BRANCH = 4
GROUP_W = 256
N_GROUPS = 10
PEER_HEADS = 8
N_KEYS = 128
PEER_KEY_DIM = 128
PEER_TOPK = 16
QBLK = 128
NA_KBLKS = 5
VMEM_LIMIT = 56 * 1024 * 1024


def _dot(a, b):
    return jnp.dot(a, b, preferred_element_type=F32)


def _dot_nt(a, b):
    return lax.dot_general(a, b, (((1,), (1,)), ((), ())), preferred_element_type=F32)


def _params(sem, vmem=VMEM_LIMIT):
    return pltpu.CompilerParams(dimension_semantics=sem, vmem_limit_bytes=vmem)


def _const_spec(shape, index_map):
    return pl.BlockSpec(shape, index_map, pipeline_mode=pl.Buffered(1))


def _mod_rows(mod_ref, k, is_ctx):
    lo, hi = k * D_MODEL, (k + 1) * D_MODEL
    return jnp.where(is_ctx, mod_ref[1:2, lo:hi], mod_ref[0:1, lo:hi])


def _norm_modulate(x, nw, shift, scale):
    ms = jnp.mean(x * x, axis=-1, keepdims=True)
    h = (x * lax.rsqrt(ms + EPS)) * nw
    return h * (1.0 + scale) + shift


def _group_mean_sq(y, width):
    w = y.shape[1]
    y2 = y * y
    r = lax.broadcasted_iota(jnp.int32, (w, w), 0) // width
    c = lax.broadcasted_iota(jnp.int32, (w, w), 1) // width
    ones = jnp.where(r == c, 1.0, 0.0).astype(BF16)
    hi = y2.astype(BF16)
    lo = (y2 - hi.astype(F32)).astype(BF16)
    return (_dot(hi, ones) + _dot(lo, ones)) * (1.0 / width)


def _rope(y, cos, sin_signed, quarter):
    w = y.shape[1]
    lane = lax.broadcasted_iota(jnp.int32, y.shape, 1)
    first = (lane % (2 * quarter)) < quarter
    rot = jnp.where(first, pltpu.roll(y, w - quarter, 1), pltpu.roll(y, quarter, 1))
    return y * cos + rot * sin_signed


def _softmax_pv_heads(scores, values, sinks=None):
    probs, denoms = [], []
    for n, s in enumerate(scores):
        m = jnp.max(s, axis=1, keepdims=True)
        if sinks is not None:
            m = jnp.maximum(m, sinks[n])
        p = jnp.exp(s - m)
        l = jnp.sum(p, axis=1, keepdims=True)
        if sinks is not None:
            l = l + jnp.exp(sinks[n] - m)
        probs.append(p.astype(BF16))
        denoms.append(l)
    return [_dot(p, v) / l for p, v, l in zip(probs, values, denoms)]


def _softmax_pv(s, v, sink=None):
    return _softmax_pv_heads([s], [v], None if sink is None else [sink])[0]


def _diff_lambda(lam_ref, lam_init):
    lf = lam_ref[...]
    a = jnp.sum(lf[0:1] * lf[1:2], axis=1, keepdims=True)
    b = jnp.sum(lf[2:3] * lf[3:4], axis=1, keepdims=True)
    return jnp.exp(a) - jnp.exp(b) + lam_init


def _diff_finish(o, subw, lam_init):
    ms = _group_mean_sq(o, D_HEAD)
    return (o * lax.rsqrt(ms + EPS)) * subw * (1.0 - lam_init)


def _mod_kernel(c_ref, w_ref, b_ref, o_ref):
    c = c_ref[...]
    s = c * jax.nn.sigmoid(c)
    o_ref[...] = jnp.dot(s, w_ref[...], precision=lax.Precision.HIGHEST,
                         preferred_element_type=F32) + b_ref[...]


def _modulation(c2, ada_w, ada_b, layer):
    n_out = ada_w.shape[2]
    tn = 1536
    return pl.pallas_call(
        _mod_kernel,
        grid=(n_out // tn,),
        in_specs=[pl.BlockSpec((8, D_MODEL), lambda j: (0, 0)),
                  pl.BlockSpec((None, D_MODEL, tn), lambda j: (layer, 0, j)),
                  pl.BlockSpec((None, 1, tn), lambda j: (layer, 0, j))],
        out_specs=pl.BlockSpec((8, tn), lambda j: (0, j)),
        out_shape=jax.ShapeDtypeStruct((8, n_out), F32),
        compiler_params=_params(("arbitrary",)),
        name="adaln_mod",
    )(c2, ada_w, ada_b.reshape(ada_b.shape[0], 1, n_out))


def _proj_kernel(x_ref, mod_ref, nw_ref, w_ref, c64_ref, s64_ref, c32_ref, s32_ref, qkw_ref,
                 o_ref, dqt_ref, dvt_ref, gqt_ref, gvt_ref, *, n_lat, tm):
    i = pl.program_id(0)
    row = i * tm + lax.broadcasted_iota(jnp.int32, (tm, 1), 0)
    is_ctx = row >= n_lat
    h = _norm_modulate(x_ref[...], nw_ref[...], _mod_rows(mod_ref, 0, is_ctx), _mod_rows(mod_ref, 1, is_ctx))
    hb = h.astype(BF16)
    c64, s64 = c64_ref[...], s64_ref[...]
    c32, s32 = c32_ref[...], s32_ref[...]
    half = GROUP_W // 2
    sc64 = D_HEAD ** -0.5
    sc32 = DIFF_QK_DIM ** -0.5

    def qk_norm(y, w):
        return (y * lax.rsqrt(_group_mean_sq(y, D_HEAD) + EPS)) * w

    for g in range(N_GROUPS):
        y = _dot(hb, w_ref[:, g * GROUP_W:(g + 1) * GROUP_W])
        if g == 0:
            y = y * sc64
        elif g == 3:
            y = _rope(y, c32, s32, DIFF_QK_DIM // 4) * sc32
            dqt_ref[...] = (y * LOG2E).T.astype(BF16)
        elif g == 5:
            dvt_ref[...] = y.T.astype(BF16)
        elif g == 4:
            y = _rope(y, c32, s32, DIFF_QK_DIM // 4)
        elif g == 6:
            y = _rope(y, c64, s64, D_HEAD // 4) * sc64
        elif g == 7:
            k = _rope(y[:, :half], c64[:, :half], s64[:, :half], D_HEAD // 4)
            y = jnp.concatenate([k, y[:, half:]], axis=1)
        elif g == 8:
            y = _rope(qk_norm(y, qkw_ref[0:1, :]), c64, s64, D_HEAD // 4) * sc64
            gqt_ref[...] = (y * LOG2E).T.astype(BF16)
        elif g == 9:
            k = _rope(qk_norm(y[:, :half], qkw_ref[1:2, :half]), c64[:, :half], s64[:, :half], D_HEAD // 4)
            gvt_ref[...] = y[:, half:].T.astype(BF16)
            y = jnp.concatenate([k, y[:, half:]], axis=1)
        o_ref[:, g * GROUP_W:(g + 1) * GROUP_W] = y.astype(BF16)


def _projection(xs, mod, nw, w_in, tabs, qkw, layer, n_lat):
    n = xs.shape[0]
    tm = _row_tile(n)
    width = N_GROUPS * GROUP_W
    row_spec = lambda w: pl.BlockSpec((tm, w), lambda i: (i, 0))
    col_spec = lambda w: pl.BlockSpec((w, tm), lambda i: (0, i))
    t_shape = lambda w: jax.ShapeDtypeStruct((w, n), BF16)
    return pl.pallas_call(
        functools.partial(_proj_kernel, n_lat=n_lat, tm=tm),
        grid=(n // tm,),
        in_specs=[row_spec(D_MODEL),
                  _const_spec((8, 6 * D_MODEL), lambda i: (0, 0)),
                  _const_spec((None, 1, D_MODEL), lambda i: (layer, 0, 0)),
                  _const_spec((None, D_MODEL, width), lambda i: (layer, 0, 0)),
                  row_spec(GROUP_W), row_spec(GROUP_W), row_spec(GROUP_W), row_spec(GROUP_W),
                  _const_spec((None, 2, GROUP_W), lambda i: (layer, 0, 0))],
        out_specs=[row_spec(width), col_spec(GROUP_W), col_spec(GROUP_W), col_spec(GROUP_W),
                   col_spec(GROUP_W // 2)],
        out_shape=[jax.ShapeDtypeStruct((n, width), BF16), t_shape(GROUP_W), t_shape(GROUP_W), t_shape(GROUP_W),
                   t_shape(GROUP_W // 2)],
        compiler_params=_params(("parallel",)),
        name="proj",
    )(xs, mod, nw, w_in, *tabs, qkw)


def _na_kernel(q_ref, *refs):
    k_refs = refs[:NA_KBLKS]
    v_refs = refs[NA_KBLKS:2 * NA_KBLKS]
    kc_ref, vc_ref, bias_ref, o_ref = refs[2 * NA_KBLKS:]
    q = q_ref[...]
    ks = [r[...] for r in k_refs] + [kc_ref[...]]
    vs = [r[...] for r in v_refs] + [vc_ref[...]]
    scores, values = [], []
    for h in range(4):
        sl = slice(h * D_HEAD, (h + 1) * D_HEAD)
        qh = q[:, sl]
        s_loc = jnp.concatenate([_dot_nt(qh, k[:, sl]) for k in ks[:NA_KBLKS]], axis=1) + bias_ref[0, h]
        scores.append(jnp.concatenate([s_loc, _dot_nt(qh, ks[NA_KBLKS][:, sl])], axis=1))
        values.append(jnp.concatenate([v[:, sl] for v in vs], axis=0))
    o_ref[...] = jnp.concatenate(_softmax_pv_heads(scores, values), axis=1).astype(BF16)


def _na_bias(rpb, n_lat):
    nb = n_lat // QBLK
    rows = n_lat // GRID_W
    rep = np.array([0, 1, 2, nb - 2, nb - 1])
    wstart = np.clip(rep - 2, 0, nb - NA_KBLKS)
    qi = np.arange(QBLK)
    kj = np.arange(NA_KBLKS * QBLK)
    qrow = 2 * rep[:, None] + qi[None, :] // GRID_W
    qcol = qi % GRID_W
    krow = 2 * wstart[:, None] + kj[None, :] // GRID_W
    kcol = kj % GRID_W
    rs = np.clip(qrow - NA_KH // 2, 0, rows - NA_KH)
    row_ok = (krow[:, None, :] >= rs[:, :, None]) & (krow[:, None, :] < rs[:, :, None] + NA_KH)
    cs = np.clip(qcol - NA_KW // 2, 0, GRID_W - NA_KW)
    col_ok = (kcol[None, :] >= cs[:, None]) & (kcol[None, :] < cs[:, None] + NA_KW)
    ok = row_ok & col_ok[None]
    gc = np.arange(GRID_W)
    cidx = np.clip(gc[None, :] - gc[:, None], -(NA_KW - 1), NA_KW - 1) + (NA_KW - 1)
    onehot = (cidx.reshape(-1)[None, :] == np.arange(2 * NA_KW - 1)[:, None]).astype(np.float32)
    table = jnp.einsum("hdc,cn->hdn", rpb.astype(F32), onehot, precision=lax.Precision.HIGHEST)
    table = table.reshape(rpb.shape[0], 2 * NA_KH - 1, GRID_W, GRID_W)
    zero = jnp.zeros((rpb.shape[0], GRID_W, GRID_W), F32)
    classes = []
    for c in range(len(rep)):
        q_rows = []
        for qr in range(QBLK // GRID_W):
            pieces = []
            for kr in range(NA_KBLKS * QBLK // GRID_W):
                d = int(2 * wstart[c] + kr - (2 * rep[c] + qr) + NA_KH - 1)
                pieces.append(table[:, d] if 0 <= d <= 2 * NA_KH - 2 else zero)
            q_rows.append(jnp.concatenate(pieces, axis=-1))
        classes.append(jnp.concatenate(q_rows, axis=-2))
    return jnp.where(ok[:, None], jnp.stack(classes), NEG_INF)


def _na_attention(qkv, bias, n_lat):
    n = qkv.shape[0]
    nb = n_lat // QBLK
    cblk = n_lat // 256

    def cls(b):
        return jnp.where(b < 2, b, jnp.where(b >= nb - 2, b - (nb - 5), 2))

    def kblk(j, col):
        return pl.BlockSpec((QBLK, GROUP_W), lambda b: (jnp.clip(b - 2, 0, nb - NA_KBLKS) + j, col))

    in_specs = ([pl.BlockSpec((QBLK, GROUP_W), lambda b: (b, 0))]
                + [kblk(j, 1) for j in range(NA_KBLKS)] + [kblk(j, 2) for j in range(NA_KBLKS)]
                + [pl.BlockSpec((256, GROUP_W), lambda b: (cblk, 1)),
                   pl.BlockSpec((256, GROUP_W), lambda b: (cblk, 2)),
                   pl.BlockSpec((1, 4, QBLK, NA_KBLKS * QBLK), lambda b: (cls(b), 0, 0, 0))])
    return pl.pallas_call(
        _na_kernel,
        grid=(nb,),
        in_specs=in_specs,
        out_specs=pl.BlockSpec((QBLK, GROUP_W), lambda b: (b, 0)),
        out_shape=jax.ShapeDtypeStruct((n, GROUP_W), BF16),
        compiler_params=_params(("parallel",)),
        name="na_attn",
    )(*([qkv] * (3 + 2 * NA_KBLKS)), bias)


def _swa_kernel(sink_ref, q_ref, km_ref, k0_ref, kp_ref, kvc_ref, o_ref, *, nb):
    b = pl.program_id(0)
    q = q_ref[...]
    blocks = [km_ref[...], k0_ref[...], kp_ref[...], kvc_ref[...]]
    row = lax.broadcasted_iota(jnp.int32, (QBLK, QBLK), 0)
    col = lax.broadcasted_iota(jnp.int32, (QBLK, QBLK), 1)
    ok_prev = jnp.logical_and(col >= row, b > 0)
    ok_next = jnp.logical_and(col <= row, b < nb - 1)
    scores, values = [], []
    for h in range(4):
        g = h // 2
        qh = q[:, h * D_HEAD:(h + 1) * D_HEAD]
        ksl = slice(g * D_HEAD, (g + 1) * D_HEAD)
        vsl = slice(GROUP_W // 2 + g * D_HEAD, GROUP_W // 2 + (g + 1) * D_HEAD)
        scores.append(jnp.concatenate([jnp.where(ok_prev, _dot_nt(qh, blocks[0][:, ksl]), NEG_INF),
                                       _dot_nt(qh, blocks[1][:, ksl]),
                                       jnp.where(ok_next, _dot_nt(qh, blocks[2][:, ksl]), NEG_INF),
                                       _dot_nt(qh, blocks[3][:, ksl])], axis=1))
        values.append(jnp.concatenate([blk[:, vsl] for blk in blocks], axis=0))
    outs = _softmax_pv_heads(scores, values, [sink_ref[h] for h in range(4)])
    o_ref[...] = jnp.concatenate(outs, axis=1).astype(BF16)


def _swa_attention(qkv, sink, n_lat):
    n = qkv.shape[0]
    nb = n_lat // QBLK
    cblk = n_lat // 256
    return pl.pallas_call(
        functools.partial(_swa_kernel, nb=nb),
        grid=(nb,),
        in_specs=[pl.BlockSpec(memory_space=pltpu.SMEM),
                  pl.BlockSpec((QBLK, GROUP_W), lambda b: (b, 6)),
                  pl.BlockSpec((QBLK, GROUP_W), lambda b: (jnp.maximum(b - 1, 0), 7)),
                  pl.BlockSpec((QBLK, GROUP_W), lambda b: (b, 7)),
                  pl.BlockSpec((QBLK, GROUP_W), lambda b: (jnp.minimum(b + 1, nb - 1), 7)),
                  pl.BlockSpec((256, GROUP_W), lambda b: (cblk, 7))],
        out_specs=pl.BlockSpec((QBLK, GROUP_W), lambda b: (b, 0)),
        out_shape=jax.ShapeDtypeStruct((n, GROUP_W), BF16),
        compiler_params=_params(("parallel",)),
        name="swa_attn",
    )(sink, qkv, qkv, qkv, qkv, qkv)


ONES_ROWS = 16
ACC_ROWS = D_HEAD + ONES_ROWS
FLASH_QUERY_CHUNK = 512


def _flash_units(units, qtu_ref, m_ref, acc_ref):
    tq = qtu_ref.shape[2]
    width = min(tq, FLASH_QUERY_CHUNK)
    units = [(u, k_lanes, vt_ext, slice(c * width, (c + 1) * width))
             for (u, k_lanes, vt_ext) in units for c in range(tq // width)]
    n = len(units)
    st, pt, alpha = {}, {}, {}
    for t in range(n + 2):
        if t < n:
            u, k_lanes, _, cols = units[t]
            st[t] = _dot(k_lanes, qtu_ref[u, :, cols])
        if 0 <= t - 1 < n:
            u, _, _, cols = units[t - 1]
            s = st.pop(t - 1)
            m_old = m_ref[u, :, cols]
            m_new = jnp.maximum(m_old, jnp.max(s, axis=0, keepdims=True))
            alpha[t - 1] = jnp.exp2(m_old - m_new)
            pt[t - 1] = jnp.exp2(s - m_new).astype(BF16)
            m_ref[u, :, cols] = m_new
        if 0 <= t - 2 < n:
            u, _, vt_ext, cols = units[t - 2]
            acc_ref[u, :, cols] = alpha.pop(t - 2) * acc_ref[u, :, cols] + _dot(vt_ext, pt.pop(t - 2))


def _flash_init(m_ref, acc_ref):
    m_ref[...] = jnp.full(m_ref.shape, NEG_INF, F32)
    acc_ref[...] = jnp.zeros(acc_ref.shape, F32)


def _with_ones(vt_h):
    return jnp.concatenate([vt_h, jnp.ones((ONES_ROWS, vt_h.shape[1]), BF16)], axis=0)


def _normalised(acc):
    return acc[:D_HEAD] / acc[D_HEAD:D_HEAD + 1]


def _diff_flash_kernel(lam_ref, subw_ref, qt_ref, k_ref, vt_ref, o_ref, qtu_ref, m_ref, acc_ref, *, lam_init):
    j = pl.program_id(1)
    lanes = 128

    @pl.when(j == 0)
    def _():
        _flash_init(m_ref, acc_ref)
        row = lax.broadcasted_iota(jnp.int32, (lanes, qt_ref.shape[1]), 0)
        for u in range(8):
            grp = qt_ref[(u // 4) * lanes:(u // 4 + 1) * lanes, :].astype(F32)
            lo = (u % 4) * DIFF_QK_DIM
            own = jnp.logical_and(row >= lo, row < lo + DIFF_QK_DIM)
            qtu_ref[u] = jnp.where(own, grp, 0.0).astype(BF16)

    k, vt = k_ref[...], vt_ref[...]
    vt_ext = [_with_ones(vt[h * D_HEAD:(h + 1) * D_HEAD]) for h in range(4)]
    _flash_units([(u, k[:, (u // 4) * lanes:(u // 4 + 1) * lanes], vt_ext[u // 2]) for u in range(8)],
                 qtu_ref, m_ref, acc_ref)

    @pl.when(j == pl.num_programs(1) - 1)
    def _():
        lam = _diff_lambda(lam_ref, lam_init)
        heads = []
        for h in range(4):
            o = _normalised(acc_ref[2 * h]) - lam * _normalised(acc_ref[2 * h + 1])
            heads.append(o * lax.rsqrt(jnp.mean(o * o, axis=0, keepdims=True) + EPS))
        ot = jnp.concatenate(heads, axis=0)
        o_ref[...] = (ot.T * subw_ref[...] * (1.0 - lam_init)).astype(BF16)


def _gqa_flash_kernel(qt_ref, k_ref, vt_ref, o_ref, qtu_ref, m_ref, acc_ref):
    j = pl.program_id(1)

    @pl.when(j == 0)
    def _():
        _flash_init(m_ref, acc_ref)
        zeros = jnp.zeros((D_HEAD, qt_ref.shape[1]), BF16)
        for h in range(4):
            qh = qt_ref[h * D_HEAD:(h + 1) * D_HEAD, :]
            qtu_ref[h] = jnp.concatenate([qh, zeros] if h < 2 else [zeros, qh], axis=0)

    k, vt = k_ref[...], vt_ref[...]
    vt_ext = [_with_ones(vt[g * D_HEAD:(g + 1) * D_HEAD]) for g in range(2)]
    _flash_units([(h, k, vt_ext[h // 2]) for h in range(4)], qtu_ref, m_ref, acc_ref)

    @pl.when(j == pl.num_programs(1) - 1)
    def _():
        ot = jnp.concatenate([_normalised(acc_ref[h]) for h in range(4)], axis=0)
        o_ref[...] = ot.T.astype(BF16)


def _flash_tiles(n, n_lat):
    tq = next(t for t in (2048, 1024, 512, 256) if n_lat % t == 0)
    tk = next(t for t in (1280, 256) if n % t == 0)
    return tq, tk


def _flash_scratch(units, tq):
    return [pltpu.VMEM((units, 128, tq), BF16), pltpu.VMEM((units, 1, tq), F32),
            pltpu.VMEM((units, ACC_ROWS, tq), F32)]


def _diff_attention(qkv, qt, vt, lam_p, subw, lam_init, n_lat):
    n = qkv.shape[0]
    tq, tk = _flash_tiles(n, n_lat)
    return pl.pallas_call(
        functools.partial(_diff_flash_kernel, lam_init=lam_init),
        grid=(n_lat // tq, n // tk),
        in_specs=[pl.BlockSpec((4, DIFF_QK_DIM), lambda i, j: (0, 0)),
                  pl.BlockSpec((1, GROUP_W), lambda i, j: (0, 0)),
                  pl.BlockSpec((GROUP_W, tq), lambda i, j: (0, i)),
                  pl.BlockSpec((tk, GROUP_W), lambda i, j: (j, 4)),
                  pl.BlockSpec((GROUP_W, tk), lambda i, j: (0, j))],
        out_specs=pl.BlockSpec((tq, GROUP_W), lambda i, j: (i, 0)),
        out_shape=jax.ShapeDtypeStruct((n, GROUP_W), BF16),
        scratch_shapes=_flash_scratch(8, tq),
        compiler_params=_params(("parallel", "arbitrary")),
        name="diff_attn",
    )(lam_p, subw, qt, qkv, vt)


def _gqa_attention(qkv, qt, vt, n_lat):
    n = qkv.shape[0]
    tq, tk = _flash_tiles(n, n_lat)
    half = GROUP_W // 2
    return pl.pallas_call(
        _gqa_flash_kernel,
        grid=(n_lat // tq, n // tk),
        in_specs=[pl.BlockSpec((GROUP_W, tq), lambda i, j: (0, i)),
                  pl.BlockSpec((tk, half), lambda i, j: (j, 9 * GROUP_W // half)),
                  pl.BlockSpec((half, tk), lambda i, j: (0, j))],
        out_specs=pl.BlockSpec((tq, GROUP_W), lambda i, j: (i, 0)),
        out_shape=jax.ShapeDtypeStruct((n, GROUP_W), BF16),
        scratch_shapes=_flash_scratch(4, tq),
        compiler_params=_params(("parallel", "arbitrary")),
        name="gqa_attn",
    )(qt, qkv, vt)


def _ctx_kernel(sink_ref, lam_ref, subw_ref, qkv_ref, a0, a1, a2, a3, o_na, o_df, o_sw, o_gq, *, lam_init):
    del a0, a1, a2, a3
    grp = lambda g: qkv_ref[:, g * GROUP_W:(g + 1) * GROUP_W]
    hs = lambda h: slice(h * D_HEAD, (h + 1) * D_HEAD)
    half = GROUP_W // 2

    q, k, v = grp(0), grp(1), grp(2)
    o_na[...] = jnp.concatenate([_softmax_pv(_dot_nt(q[:, hs(h)], k[:, hs(h)]), v[:, hs(h)])
                                 for h in range(4)], axis=1).astype(BF16)

    q, k, v = grp(3), grp(4), grp(5)
    lam = _diff_lambda(lam_ref, lam_init)
    heads = []
    for h in range(4):
        maps = []
        for m in range(2):
            qs = slice((2 * h + m) * DIFF_QK_DIM, (2 * h + m + 1) * DIFF_QK_DIM)
            maps.append(_softmax_pv(_dot_nt(q[:, qs], k[:, qs]), v[:, hs(h)]))
        heads.append(maps[0] - lam * maps[1])
    o_df[...] = _diff_finish(jnp.concatenate(heads, axis=1), subw_ref[...], lam_init).astype(BF16)

    for q_g, kv_g, out, use_sink in ((6, 7, o_sw, True), (8, 9, o_gq, False)):
        q, kv = grp(q_g), grp(kv_g)
        outs = []
        for h in range(4):
            g = h // 2
            s = _dot_nt(q[:, hs(h)], kv[:, hs(g)])
            vh = kv[:, half + g * D_HEAD:half + (g + 1) * D_HEAD]
            outs.append(_softmax_pv(s, vh, sink=sink_ref[h] if use_sink else None))
        out[...] = jnp.concatenate(outs, axis=1).astype(BF16)


def _ctx_attention(qkv, ys, sink, lam_p, subw, lam_init, n_lat):
    n = qkv.shape[0]
    n_ctx = n - n_lat
    cblk = n_lat // n_ctx
    y_spec = pl.BlockSpec((n_ctx, GROUP_W), lambda i: (cblk, 0))
    any_spec = pl.BlockSpec(memory_space=pl.ANY)
    return pl.pallas_call(
        functools.partial(_ctx_kernel, lam_init=lam_init),
        grid=(1,),
        in_specs=[pl.BlockSpec(memory_space=pltpu.SMEM),
                  pl.BlockSpec((4, DIFF_QK_DIM), lambda i: (0, 0)),
                  pl.BlockSpec((1, GROUP_W), lambda i: (0, 0)),
                  pl.BlockSpec((n_ctx, N_GROUPS * GROUP_W), lambda i: (cblk, 0)),
                  any_spec, any_spec, any_spec, any_spec],
        out_specs=[y_spec] * 4,
        out_shape=[jax.ShapeDtypeStruct((n, GROUP_W), BF16)] * 4,
        input_output_aliases={4: 0, 5: 1, 6: 2, 7: 3},
        compiler_params=_params(("arbitrary",)),
        name="ctx_attn",
    )(sink, lam_p, subw, qkv, *ys)


def _merge_kernel(x_ref, mod_ref, nw_ref, y0, y1, y2, y3, wg_ref, bg_ref, wb_ref, wo_ref, o_ref, *, n_lat, tm):
    i = pl.program_id(0)
    row = i * tm + lax.broadcasted_iota(jnp.int32, (tm, 1), 0)
    is_ctx = row >= n_lat
    x = x_ref[...]
    h = _norm_modulate(x, nw_ref[...], _mod_rows(mod_ref, 0, is_ctx), _mod_rows(mod_ref, 1, is_ctx))
    hb = h.astype(BF16)
    acc = jnp.zeros((tm, D_MODEL), F32)
    for n, y_ref in enumerate((y0, y1, y2, y3)):
        cols = slice(n * D_MODEL, (n + 1) * D_MODEL)
        gate = jax.nn.sigmoid(_dot(hb, wg_ref[:, cols]) + bg_ref[:, cols])
        acc = acc + gate * _dot(y_ref[...], wb_ref[n])
    out = _dot(acc.astype(BF16), wo_ref[...])
    o_ref[...] = x + _mod_rows(mod_ref, 2, is_ctx) * out


def _merge(xs, mod, nw, ys, w_gate, b_gate, w_branch, w_out, layer, n_lat, n_rows):
    tm = _row_tile(n_rows)
    row_spec = lambda w: pl.BlockSpec((tm, w), lambda i: (i, 0))
    return pl.pallas_call(
        functools.partial(_merge_kernel, n_lat=n_lat, tm=tm),
        grid=(n_rows // tm,),
        in_specs=[row_spec(D_MODEL),
                  _const_spec((8, 6 * D_MODEL), lambda i: (0, 0)),
                  _const_spec((None, 1, D_MODEL), lambda i: (layer, 0, 0)),
                  row_spec(GROUP_W), row_spec(GROUP_W), row_spec(GROUP_W), row_spec(GROUP_W),
                  _const_spec((None, D_MODEL, N_BRANCH * D_MODEL), lambda i: (layer, 0, 0)),
                  _const_spec((None, 1, N_BRANCH * D_MODEL), lambda i: (layer, 0, 0)),
                  _const_spec((None, N_BRANCH, GROUP_W, D_MODEL), lambda i: (layer, 0, 0, 0)),
                  _const_spec((None, D_MODEL, D_MODEL), lambda i: (layer, 0, 0))],
        out_specs=row_spec(D_MODEL),
        out_shape=jax.ShapeDtypeStruct((n_rows, D_MODEL), F32),
        compiler_params=_params(("parallel",)),
        name="merge",
    )(xs, mod, nw, *ys, w_gate, b_gate, w_branch, w_out)


PEER_UNRANKED = 64.0


def _top_desc(ref, count, rank_ref=None):
    vals = []
    if rank_ref is not None:
        rank_ref[...] = jnp.full(rank_ref.shape, PEER_UNRANKED, F32)
    for r in range(count):
        cur = ref[...]
        m = jnp.max(cur, axis=0, keepdims=True)
        vals.append(m)
        hit = cur == m
        if rank_ref is not None:
            rank_ref[...] = jnp.where(hit, float(r), rank_ref[...])
        if r + 1 < count:
            ref[...] = jnp.where(hit, -jnp.inf, cur)
    return vals


def _gelu_tanh(x):
    y2 = (2.0 * math.sqrt(2.0 / math.pi)) * (x + 0.044715 * (x * x * x))
    return x / (1.0 + jnp.exp(-y2))


PEER_EXTRACT = PEER_TOPK + 1
_PAIR_RANKS = [(a, b) for a in range(PEER_EXTRACT) for b in range(PEER_EXTRACT) if (a + 1) * (b + 1) <= PEER_EXTRACT]
_N_CAND = -(-len(_PAIR_RANKS) // 8) * 8
PEER_TOKEN_CHUNK = 256
PEER_EXPERT_BLOCK = 256


def _peer_kernel(x_ref, mod_ref, nw_ref, wqt_ref, keys_ref, u_ref, vt_ref, fw_ref, base_ref, o_ref,
                 h2t_s, cnt_s, e1_s, rank2_s, e2_s, work_s, rank_s, cand_s, p_s, acc_s,
                 *, grp, tm, te, final_norm):
    del base_ref
    e = pl.program_id(1)
    mod_row = lambda k: mod_ref[grp:grp + 1, k * D_MODEL:(k + 1) * D_MODEL]

    @pl.when(e == 0)
    def _prepare():
        h2 = _norm_modulate(x_ref[...], nw_ref[...], mod_row(3), mod_row(4))
        h2t_s[...] = h2.T.astype(BF16)
        hbt = h2t_s[...]
        cand_s[...] = jnp.full(cand_s.shape, -jnp.inf, F32)
        for hh in range(PEER_HEADS):
            tops, scores = [], []
            for p in range(2):
                idx = 2 * hh + p
                qt = _dot(wqt_ref[idx * PEER_KEY_DIM:(idx + 1) * PEER_KEY_DIM, :], hbt)
                s = _dot(keys_ref[p, hh], qt.astype(BF16))
                scores.append(s)
                work_s[...] = s
                tops.append(_top_desc(work_s, PEER_EXTRACT, rank_s if p == 1 else None))
            for c, (ra, rb) in enumerate(_PAIR_RANKS):
                cand_s[c:c + 1, :] = tops[0][ra] + tops[1][rb]
            best = _top_desc(cand_s, PEER_EXTRACT)
            z = jnp.zeros_like(best[0])
            for bk in best[:PEER_TOPK]:
                z = z + jnp.exp(bk - best[0])
            tau = 0.5 * (best[PEER_TOPK - 1] + best[PEER_TOPK])
            need = tau - scores[0]
            cnt = jnp.zeros_like(need)
            for r, a2 in enumerate(tops[1]):
                cnt = jnp.where(a2 >= need, float(r + 1), cnt)
            cnt_s[hh] = cnt
            rank2_s[hh] = rank_s[...].astype(BF16)
            e1_s[hh] = jnp.exp(scores[0] - tops[0][0]) / z
            e2_s[hh] = jnp.exp(scores[1] - tops[1][0]).astype(BF16)
        acc_s[...] = jnp.zeros(acc_s.shape, F32)

    n_blk = te // PEER_EXPERT_BLOCK
    per_blk = PEER_EXPERT_BLOCK // N_KEYS
    span = lambda b: slice(b * PEER_EXPERT_BLOCK, (b + 1) * PEER_EXPERT_BLOCK)
    acts = {0: _dot(u_ref[span(0), :], h2t_s[...])}
    for b in range(n_blk):
        if b + 1 < n_blk:
            acts[b + 1] = _dot(u_ref[span(b + 1), :], h2t_s[...])
        a = acts.pop(b)
        for ii in range(per_blk):
            i = e * (te // N_KEYS) + b * per_blk + ii
            rows = slice(ii * N_KEYS, (ii + 1) * N_KEYS)
            prow = slice(b * PEER_EXPERT_BLOCK + ii * N_KEYS, b * PEER_EXPERT_BLOCK + (ii + 1) * N_KEYS)
            for tc in range(tm // PEER_TOKEN_CHUNK):
                cols = slice(tc * PEER_TOKEN_CHUNK, (tc + 1) * PEER_TOKEN_CHUNK)
                w = jnp.zeros((N_KEYS, PEER_TOKEN_CHUNK), BF16)
                for hh in range(PEER_HEADS):
                    sel = rank2_s[hh, :, cols] < cnt_s[hh, pl.ds(i, 1), cols].astype(BF16)
                    gate = jnp.where(sel, e2_s[hh, :, cols], jnp.zeros((), BF16))
                    w = w + gate * e1_s[hh, pl.ds(i, 1), cols].astype(BF16)
                p_s[prow, cols] = w * _gelu_tanh(a[rows, cols]).astype(BF16)
    acc_s[...] += _dot(vt_ref[...], p_s[...])

    @pl.when(e == pl.num_programs(1) - 1)
    def _finish():
        xn = x_ref[...] + mod_row(5) * acc_s[...].T
        if final_norm:
            ms = jnp.mean(xn * xn, axis=-1, keepdims=True)
            xn = (xn * lax.rsqrt(ms + EPS)) * fw_ref[...]
        o_ref[...] = xn


def _peer_call(xs, base, mod, nw, wq_t, keys, u, v_t, final_w, layer, grp, row0, n_rows, n_out, final_norm):
    tm = 512 if n_rows % 512 == 0 else 256
    te = 2048
    n_exp = u.shape[1]
    blk0 = row0 // tm
    hs = (PEER_HEADS, N_KEYS, tm)
    row_spec = pl.BlockSpec((tm, D_MODEL), lambda t, e: (blk0 + t, 0))
    return pl.pallas_call(
        functools.partial(_peer_kernel, grp=grp, tm=tm, te=te, final_norm=final_norm),
        grid=(n_rows // tm, n_exp // te),
        in_specs=[row_spec,
                  _const_spec((8, 6 * D_MODEL), lambda t, e: (0, 0)),
                  _const_spec((None, 1, D_MODEL), lambda t, e: (layer, 0, 0)),
                  _const_spec((None, 2 * PEER_HEADS * PEER_KEY_DIM, D_MODEL), lambda t, e: (layer, 0, 0)),
                  _const_spec((None, 2, PEER_HEADS, N_KEYS, PEER_KEY_DIM), lambda t, e: (layer, 0, 0, 0, 0)),
                  pl.BlockSpec((None, te, D_MODEL), lambda t, e: (layer, e, 0)),
                  pl.BlockSpec((None, D_MODEL, te), lambda t, e: (layer, 0, e)),
                  _const_spec((1, D_MODEL), lambda t, e: (0, 0)),
                  pl.BlockSpec(memory_space=pl.ANY)],
        out_specs=row_spec,
        out_shape=jax.ShapeDtypeStruct((n_out, D_MODEL), F32),
        input_output_aliases={} if base is xs else {8: 0},
        scratch_shapes=[pltpu.VMEM((D_MODEL, tm), BF16),
                        pltpu.VMEM(hs, F32), pltpu.VMEM(hs, F32), pltpu.VMEM(hs, BF16), pltpu.VMEM(hs, BF16),
                        pltpu.VMEM((N_KEYS, tm), F32), pltpu.VMEM((N_KEYS, tm), F32),
                        pltpu.VMEM((_N_CAND, tm), F32),
                        pltpu.VMEM((te, tm), BF16),
                        pltpu.VMEM((D_MODEL, tm), F32)],
        compiler_params=_params(("parallel", "arbitrary")),
        name="peer",
    )(xs, mod, nw, wq_t, keys, u, v_t, final_w, base)


def _peer(xs, mod, nw, wq_t, keys, u, v_t, final_w, layer, n_lat, with_ctx, final_norm):
    n_out = xs.shape[0] if with_ctx else n_lat
    out = _peer_call(xs, xs, mod, nw, wq_t, keys, u, v_t, final_w, layer, 0, 0, n_lat, n_out, final_norm)
    if with_ctx:
        out = _peer_call(xs, out, mod, nw, wq_t, keys, u, v_t, final_w, layer, 1, n_lat, xs.shape[0] - n_lat,
                         n_out, final_norm)
    return out


def _row_tile(n):
    for tm in (640, 512, 256):
        if n % tm == 0:
            return tm
    raise ValueError(f"unsupported row count {n}")


def _rope_tables(n_lat, n_ctx, dim):
    t = jnp.arange(n_lat, dtype=jnp.int32)
    row = (t // GRID_W).astype(F32)
    col = (t % GRID_W).astype(F32)
    n_freq = dim // 4
    inv_freq = ROPE_THETA ** (-jnp.arange(n_freq, dtype=F32) / n_freq)
    ang_r = row[:, None] * inv_freq[None, :]
    ang_c = col[:, None] * inv_freq[None, :]
    ang = jnp.concatenate([ang_r, ang_r, ang_c, ang_c], axis=-1)
    sign = np.where((np.arange(dim) % (dim // 2)) < dim // 4, -1.0, 1.0).astype(np.float32)
    cos = jnp.concatenate([jnp.cos(ang), jnp.ones((n_ctx, dim), F32)], axis=0)
    sin = jnp.concatenate([jnp.sin(ang) * sign, jnp.zeros((n_ctx, dim), F32)], axis=0)
    reps = GROUP_W // dim
    return jnp.tile(cos, (1, reps)), jnp.tile(sin, (1, reps))


def kernel(x, c, ctx, c_ctx, norm1_w, norm2_w, ada_w, ada_b, w_in, na_rpb, diff_lam, diff_subln_w, swa_sink,
           gqa_qk_norm_w, w_branch, w_gate, b_gate, w_out, peer_wq, peer_keys, peer_u, peer_v, final_norm_w):
    batch, n_lat, d = x.shape
    n_ctx = ctx.shape[1]
    depth = ada_w.shape[0]
    assert batch == 1 and d == D_MODEL and n_ctx == 256 and n_lat % 512 == 0 and n_lat >= NA_KBLKS * QBLK
    n = n_lat + n_ctx

    xs = jnp.concatenate([x[0], ctx[0]], axis=0)
    c2 = jnp.zeros((8, d), F32).at[0].set(c[0]).at[1].set(c_ctx)
    tabs = _rope_tables(n_lat, n_ctx, D_HEAD) + _rope_tables(n_lat, n_ctx, DIFF_QK_DIM)
    nw1 = norm1_w.reshape(depth, 1, d)
    nw2 = norm2_w.reshape(depth, 1, d)
    qkw = jnp.tile(gqa_qk_norm_w, (1, 1, GROUP_W // D_HEAD))
    subw = jnp.tile(diff_subln_w, (1, GROUP_W // D_HEAD))
    w_in_b = w_in.astype(BF16)
    w_gate_b = w_gate.astype(BF16)
    w_branch_b = w_branch.astype(BF16)
    w_out_b = w_out.astype(BF16)
    b_gate3 = b_gate.reshape(depth, 1, -1)
    wq_t = jnp.swapaxes(peer_wq, 1, 2).astype(BF16)
    keys_b = peer_keys.astype(BF16)
    u_b = peer_u.astype(BF16)
    v_t = jnp.swapaxes(peer_v, 1, 2).astype(BF16)
    final_w = final_norm_w.reshape(1, d)

    for l in range(depth):
        last = l == depth - 1
        lam_init = 0.8 - 0.6 * math.exp(-0.3 * l)
        n_rows = n_lat if last else n
        mod = _modulation(c2, ada_w, ada_b, l)
        qkv, dqt, dvt, gqt, gvt = _projection(xs, mod, nw1, w_in_b, tabs, qkw, l, n_lat)
        ys = (_na_attention(qkv, _na_bias(na_rpb[l], n_lat), n_lat),
              _diff_attention(qkv, dqt, dvt, diff_lam[l], subw[l:l + 1], lam_init, n_lat),
              _swa_attention(qkv, swa_sink[l], n_lat),
              _gqa_attention(qkv, gqt, gvt, n_lat))
        if not last:
            ys = _ctx_attention(qkv, ys, swa_sink[l], diff_lam[l], subw[l:l + 1], lam_init, n_lat)
        xs = _merge(xs, mod, nw1, ys, w_gate_b, b_gate3, w_branch_b, w_out_b, l, n_lat, n_rows)
        xs = _peer(xs, mod, nw2, wq_t, keys_b, u_b, v_t, final_w, l, n_lat, not last, last)
    return xs[:n_lat].reshape(batch, n_lat, d)
```
